```python
import math
import jax, jax.numpy as jnp
from jax import lax
import numpy as np

D_MODEL = 1024
BATCH = 16
SEQ = 2048
DEPTH = 4

GRID_W = 64
NA_HEAD_DIM = 64
NA_HEADS = (D_MODEL // 2) // NA_HEAD_DIM
NA_WIN_H_MAX = 8
NA_WIN_W = 16
NA_QBLOCK_W = 16
NA_KBLOCK_W = NA_QBLOCK_W + NA_WIN_W
ML_HEADS = 4
ML_HEAD_DIM = (D_MODEL // 2) // ML_HEADS
ML_CHUNK = 64
D_A = NA_HEADS * NA_HEAD_DIM
D_B = ML_HEADS * ML_HEAD_DIM
D_MIX = D_A + D_B
AB_IN = 3 * D_A + 4 * D_B + 4 * ML_HEADS
CONV_W = 3
D_FF = ((8 * D_MODEL // 3 + 255) // 256) * 256
N_EVEN = (DEPTH + 1) // 2
N_ODD = DEPTH // 2
DEEPNORM_ALPHA = (2 * DEPTH) ** 0.25
DEEPNORM_BETA = (8 * DEPTH) ** -0.25
LN_EPS = 1e-5

kernel_name = "hybrid_natten_mlstm_shortconv_macaron"


def layer_norm(x, g, b):
    xf = x.astype(jnp.float32)
    mu = jnp.mean(xf, axis=-1, keepdims=True)
    var = jnp.mean(jnp.square(xf - mu), axis=-1, keepdims=True)
    y = (xf - mu) * lax.rsqrt(var + LN_EPS)
    return (y * g.astype(jnp.float32) + b.astype(jnp.float32)).astype(x.dtype)


def swiglu_ffn(x, w_in, w_out):
    g, u = jnp.split(x @ w_in, 2, axis=-1)
    return (jax.nn.silu(g) * u) @ w_out


def neighbourhood_attention(q, k, v, rpb):
    B, S, H, dh = q.shape
    rows = S // GRID_W
    wh = min(NA_WIN_H_MAX, rows)
    nqb = GRID_W // NA_QBLOCK_W
    r = np.arange(rows)
    row_start = np.clip(r - wh // 2, 0, rows - wh)
    row_idx = row_start[:, None] + np.arange(wh)[None, :]
    c = np.arange(GRID_W)
    col_start = np.clip(c - NA_WIN_W // 2, 0, GRID_W - NA_WIN_W)
    kb_start = np.clip(np.arange(nqb) * NA_QBLOCK_W - NA_WIN_W // 2, 0, GRID_W - NA_KBLOCK_W)
    col_idx = kb_start[:, None] + np.arange(NA_KBLOCK_W)[None, :]
    qcol = c.reshape(nqb, NA_QBLOCK_W)
    qstart = col_start.reshape(nqb, NA_QBLOCK_W)[..., None]
    kcol = col_idx[:, None, :]
    col_ok = (kcol >= qstart) & (kcol < qstart + NA_WIN_W)
    dr = row_idx - r[:, None] + (NA_WIN_H_MAX - 1)
    dc = np.clip(kcol - qcol[..., None] + (NA_WIN_W - 1), 0, 2 * NA_WIN_W - 2)
    bias = rpb[:, dr[:, None, None, :, None], dc[None, :, :, None, :]]

    qg = q.reshape(B, rows, nqb, NA_QBLOCK_W, H, dh).transpose(0, 4, 1, 2, 3, 5)
    kg = k.reshape(B, rows, GRID_W, H, dh).transpose(0, 3, 1, 2, 4)
    vg = v.reshape(B, rows, GRID_W, H, dh).transpose(0, 3, 1, 2, 4)
    ri = row_idx[:, :, None, None]
    ci = col_idx[None, None, :, :]
    kb = kg[:, :, ri, ci]
    vb = vg[:, :, ri, ci]
    s = jnp.einsum('bhrjqd,bhrijkd->bhrjqik', qg, kb).astype(jnp.float32)
    s = s + bias.astype(jnp.float32)[None]
    s = jnp.where(col_ok[:, :, None, :], s, -jnp.inf)
    p = jax.nn.softmax(s, axis=(-2, -1)).astype(v.dtype)
    o = jnp.einsum('bhrjqik,bhrijkd->bhrjqd', p, vb)
    return o.transpose(0, 2, 3, 4, 1, 5).reshape(B, S, H * dh)


def mlstm_chunkwise(q, k, v, i_pre, f_pre):
    B, H, S, d = q.shape
    L = ML_CHUNK
    nc = S // L
    logf = jax.nn.log_sigmoid(f_pre)

    def to_chunks(a):
        return jnp.moveaxis(a.reshape(B, H, nc, L, *a.shape[3:]), 2, 0)

    xs = (to_chunks(q), to_chunks(k), to_chunks(v), to_chunks(i_pre), to_chunks(logf))
    lower = jnp.asarray(np.tril(np.ones((L, L), dtype=bool)))

    def step(carry, xs_c):
        C, n, m = carry
        qx, kx, vx, ix, fx = xs_c
        b = jnp.cumsum(fx, axis=-1)
        g = b[..., -1]
        dmat = jnp.where(lower, b[..., :, None] - b[..., None, :] + ix[..., None, :], -jnp.inf)
        inter = b + m[..., None]
        m_t = jnp.maximum(inter, jnp.max(dmat, axis=-1))
        w = jnp.exp(dmat - m_t[..., None])
        s_inter = jnp.exp(inter - m_t)
        qk = jnp.einsum('bhtd,bhsd->bhts', qx, kx) * w
        num = jnp.einsum('bhts,bhse->bhte', qk, vx) + s_inter[..., None] * jnp.einsum('bhtd,bhde->bhte', qx, C)
        den = jnp.sum(qk, axis=-1) + s_inter * jnp.einsum('bhtd,bhd->bht', qx, n)
        h = num / jnp.maximum(jnp.abs(den), jnp.exp(-m_t))[..., None]
        a = g[..., None] - b + ix
        m_new = jnp.maximum(g + m, jnp.max(a, axis=-1))
        decay = jnp.exp(g + m - m_new)
        wa = jnp.exp(a - m_new[..., None])
        C_new = decay[..., None, None] * C + jnp.einsum('bhs,bhsd,bhse->bhde', wa, kx, vx)
        n_new = decay[..., None] * n + jnp.einsum('bhs,bhsd->bhd', wa, kx)
        return (C_new, n_new, m_new), h

    init = (jnp.zeros((B, H, d, d), jnp.float32), jnp.zeros((B, H, d), jnp.float32),
            jnp.zeros((B, H), jnp.float32))
    _, hs = lax.scan(step, init, xs)
    return jnp.moveaxis(hs, 0, 2).reshape(B, H, S, d)


def na_mlstm_mixer(x, w_in, gate_b, rpb, gn_g, w_out):
    B, S, _ = x.shape
    cuts = [D_A, 2 * D_A, 3 * D_A, 3 * D_A + D_B, 3 * D_A + 2 * D_B, 3 * D_A + 3 * D_B, 3 * D_A + 4 * D_B]
    qa, ka, va, qb, kb, vb, ob, gates = jnp.split(x @ w_in, cuts, axis=-1)
    ha = lambda t: t.reshape(B, S, NA_HEADS, NA_HEAD_DIM)
    ya = neighbourhood_attention(ha(qa) * (NA_HEAD_DIM ** -0.5), ha(ka), ha(va), rpb)
    hb = lambda t: t.reshape(B, S, ML_HEADS, ML_HEAD_DIM).transpose(0, 2, 1, 3).astype(jnp.float32)
    q_m, k_m, v_m = hb(qb), hb(kb) * (ML_HEAD_DIM ** -0.5), hb(vb)
    g = gates.astype(jnp.float32).reshape(B, S, 4, ML_HEADS) + gate_b.astype(jnp.float32)
    i_f, f_f, i_b, f_b = g.transpose(2, 0, 3, 1)
    h_fwd = mlstm_chunkwise(q_m, k_m, v_m, i_f, f_f)
    flip = lambda t: jnp.flip(t, axis=2)
    h_bwd = flip(mlstm_chunkwise(flip(q_m), flip(k_m), flip(v_m), flip(i_b), flip(f_b)))
    h = h_fwd + h_bwd
    mu = jnp.mean(h, axis=-1, keepdims=True)
    var = jnp.mean(jnp.square(h - mu), axis=-1, keepdims=True)
    hn = ((h - mu) * lax.rsqrt(var + LN_EPS)).transpose(0, 2, 1, 3).reshape(B, S, D_B)
    yb = (jax.nn.sigmoid(ob.astype(jnp.float32)) * hn * gn_g.astype(jnp.float32)).astype(x.dtype)
    return jnp.concatenate([ya, yb], axis=-1) @ w_out


def short_conv_mixer(x, w_in, conv_w, conv_b, w_out):
    bg, cg, h = jnp.split(x @ w_in, 3, axis=-1)
    u = cg * h
    y = lax.conv_general_dilated(u, conv_w[:, None, :].astype(u.dtype), window_strides=(1,),
                                 padding=((CONV_W // 2, CONV_W // 2),),
                                 dimension_numbers=('NWC', 'WIO', 'NWC'),
                                 feature_group_count=D_MODEL) + conv_b
    return (bg * y) @ w_out


def setup_inputs(seed: int = 0) -> dict:
    key = jax.random.key(seed)
    ks = jax.random.split(key, 16)
    nrm = jax.random.normal
    x = nrm(ks[0], (BATCH, SEQ, D_MODEL), jnp.float32)
    ln_g = 1.0 + 0.02 * nrm(ks[1], (DEPTH, 3, D_MODEL), jnp.float32)
    ln_b = 0.02 * nrm(ks[2], (DEPTH, 3, D_MODEL), jnp.float32)
    ffn_w_in = nrm(ks[3], (DEPTH, 2, D_MODEL, 2 * D_FF), jnp.float32) * D_MODEL ** -0.5
    ffn_w_out = nrm(ks[4], (DEPTH, 2, D_FF, D_MODEL), jnp.float32) * (D_FF ** -0.5 * DEEPNORM_BETA)
    ab_w_in = nrm(ks[5], (N_EVEN, D_MODEL, AB_IN), jnp.float32) * D_MODEL ** -0.5
    f_bias = jnp.linspace(3.0, 6.0, ML_HEADS, dtype=jnp.float32)
    zero_h = jnp.zeros((ML_HEADS,), jnp.float32)
    ab_gate_b = jnp.stack([zero_h, f_bias, zero_h, f_bias])[None] + 0.1 * nrm(ks[6], (N_EVEN, 4, ML_HEADS), jnp.float32)
    na_rpb = 0.02 * nrm(ks[7], (N_EVEN, NA_HEADS, 2 * NA_WIN_H_MAX - 1, 2 * NA_WIN_W - 1), jnp.float32)
    ml_gn_g = 1.0 + 0.02 * nrm(ks[8], (N_EVEN, D_B), jnp.float32)
    ab_w_out = nrm(ks[9], (N_EVEN, D_MIX, D_MODEL), jnp.float32) * (D_MIX ** -0.5 * DEEPNORM_BETA)
    sc_w_in = nrm(ks[10], (N_ODD, D_MODEL, 3 * D_MODEL), jnp.float32) * D_MODEL ** -0.5
    sc_conv_w = nrm(ks[11], (N_ODD, CONV_W, D_MODEL), jnp.float32) * CONV_W ** -0.5
    sc_conv_b = 0.02 * nrm(ks[12], (N_ODD, D_MODEL), jnp.float32)
    sc_w_out = nrm(ks[13], (N_ODD, D_MODEL, D_MODEL), jnp.float32) * (D_MODEL ** -0.5 * DEEPNORM_BETA)
    return {"x": x, "ln_g": ln_g, "ln_b": ln_b, "ffn_w_in": ffn_w_in, "ffn_w_out": ffn_w_out,
            "ab_w_in": ab_w_in, "ab_gate_b": ab_gate_b, "na_rpb": na_rpb, "ml_gn_g": ml_gn_g,
            "ab_w_out": ab_w_out, "sc_w_in": sc_w_in, "sc_conv_w": sc_conv_w,
            "sc_conv_b": sc_conv_b, "sc_w_out": sc_w_out}


def reference(x, ln_g, ln_b, ffn_w_in, ffn_w_out, ab_w_in, ab_gate_b, na_rpb, ml_gn_g,
              ab_w_out, sc_w_in, sc_conv_w, sc_conv_b, sc_w_out):
    a = DEEPNORM_ALPHA
    for layer in range(DEPTH):
        x = layer_norm(a * x + 0.5 * swiglu_ffn(x, ffn_w_in[layer, 0], ffn_w_out[layer, 0]),
                       ln_g[layer, 0], ln_b[layer, 0])
        if layer % 2 == 0:
            e = layer // 2
            mix = na_mlstm_mixer(x, ab_w_in[e], ab_gate_b[e], na_rpb[e], ml_gn_g[e], ab_w_out[e])
        else:
            o = layer // 2
            mix = short_conv_mixer(x, sc_w_in[o], sc_conv_w[o], sc_conv_b[o], sc_w_out[o])
        x = layer_norm(a * x + mix, ln_g[layer, 1], ln_b[layer, 1])
        x = layer_norm(a * x + 0.5 * swiglu_ffn(x, ffn_w_in[layer, 1], ffn_w_out[layer, 1]),
                       ln_g[layer, 2], ln_b[layer, 2])
    return x
```

```python
import functools

import numpy as np
import jax
import jax.numpy as jnp
from jax import lax
from jax.experimental import pallas as pl
from jax.experimental.pallas import tpu as pltpu

D_MODEL = 1024
DEPTH = 4
GRID_W = 64
NA_HEAD_DIM = 64
NA_HEADS = 8
NA_WIN_H = 8
NA_WIN_W = 16
ML_HEADS = 4
ML_HEAD_DIM = 128
D_A = NA_HEADS * NA_HEAD_DIM
D_B = ML_HEADS * ML_HEAD_DIM
D_FF = 2816
DEEPNORM_ALPHA = (2 * DEPTH) ** 0.25
LN_EPS = 1e-5

LANES = 128
SUBLANES = 8
VMEM_LIMIT = 52 * 1024 * 1024

FFN_TM = 1024
FFN_TF = 256
PROJ_TM = 512
ML_CHUNK = 256
NA_ROWS_PER_BLOCK = 2
NA_Q = NA_ROWS_PER_BLOCK * GRID_W
NA_KROWS = 10
NA_K = NA_KROWS * GRID_W
NA_MASKED = -1e30

BF16 = jnp.bfloat16
F32 = jnp.float32


def _dot(a, b):
    return jnp.dot(a, b, preferred_element_type=F32)


def _layer_norm(y, g, b):
    mu = jnp.mean(y, axis=-1, keepdims=True)
    yc = y - mu
    var = jnp.mean(yc * yc, axis=-1, keepdims=True)
    return yc * lax.rsqrt(var + LN_EPS) * g + b


def _params(*sem):
    return pltpu.CompilerParams(dimension_semantics=sem, vmem_limit_bytes=VMEM_LIMIT)


def _ffn_kernel(x_ref, wg_ref, wu_ref, wo_ref, g_ref, b_ref, o_ref, xb_ref, acc_ref):
    j = pl.program_id(1)

    @pl.when(j == 0)
    def _():
        xb_ref[...] = x_ref[...].astype(BF16)
        acc_ref[...] = jnp.zeros_like(acc_ref)

    xb = xb_ref[...]
    gate = _dot(xb, wg_ref[...])
    up = _dot(xb, wu_ref[...])
    act = (gate * jax.nn.sigmoid(gate) * up).astype(BF16)
    acc_ref[...] += _dot(act, wo_ref[...])

    @pl.when(j == pl.num_programs(1) - 1)
    def _():
        y = DEEPNORM_ALPHA * x_ref[...] + 0.5 * acc_ref[...]
        o_ref[...] = _layer_norm(y, g_ref[...], b_ref[...])


def _ffn_ln(x, w_in, w_out, ln_g, ln_b, layer, k, ln_idx):
    n = x.shape[0]
    nj = D_FF // FFN_TF
    return pl.pallas_call(
        _ffn_kernel,
        grid=(n // FFN_TM, nj),
        in_specs=[
            pl.BlockSpec((FFN_TM, D_MODEL), lambda i, j: (i, 0)),
            pl.BlockSpec((None, None, D_MODEL, FFN_TF), lambda i, j: (layer, k, 0, j)),
            pl.BlockSpec((None, None, D_MODEL, FFN_TF), lambda i, j: (layer, k, 0, nj + j)),
            pl.BlockSpec((None, None, FFN_TF, D_MODEL), lambda i, j: (layer, k, j, 0)),
            pl.BlockSpec((None, 1, D_MODEL), lambda i, j: (ln_idx, 0, 0)),
            pl.BlockSpec((None, 1, D_MODEL), lambda i, j: (ln_idx, 0, 0)),
        ],
        out_specs=pl.BlockSpec((FFN_TM, D_MODEL), lambda i, j: (i, 0)),
        out_shape=jax.ShapeDtypeStruct((n, D_MODEL), F32),
        scratch_shapes=[pltpu.VMEM((FFN_TM, D_MODEL), BF16), pltpu.VMEM((FFN_TM, D_MODEL), F32)],
        compiler_params=_params("parallel", "arbitrary"),
        name="ffn_ln",
    )(x, w_in, w_in, w_out, ln_g, ln_b)


_NN_GROUPS = 5
_GATE_ROWS = ML_HEADS * SUBLANES


def _ab_proj_kernel(x_ref, wnn_ref, wnt_ref, gb_ref,
                    qa_ref, va_ref, qb_ref, vb_ref, ob_ref, kat_ref, kbt_ref, gt_ref):
    xb = x_ref[...].astype(BF16)

    def nn(g):
        return _dot(xb, wnn_ref[:, g * D_A:(g + 1) * D_A])

    def nt(lo, hi):
        return lax.dot_general(wnt_ref[lo:hi, :], xb, (((1,), (1,)), ((), ())),
                               preferred_element_type=F32)

    qa_ref[...] = (nn(0) * (NA_HEAD_DIM ** -0.5)).astype(BF16)
    va_ref[...] = nn(1).astype(BF16)
    qb_ref[...] = nn(2).astype(BF16)
    vb_ref[...] = nn(3).astype(BF16)
    ob_ref[...] = nn(4)
    kat = nt(0, D_A).astype(BF16)
    kbt = nt(D_A, D_A + D_B).astype(BF16)
    gt = nt(D_A + D_B, D_A + D_B + _GATE_ROWS) + gb_ref[...]
    for c in range(PROJ_TM // LANES):
        kat_ref[c] = kat[:, c * LANES:(c + 1) * LANES]
        kbt_ref[c] = kbt[:, c * LANES:(c + 1) * LANES]
        gt_ref[c] = gt[:, c * LANES:(c + 1) * LANES]


def _ab_proj(x, wnn, wnt, gate_b, e, batch, seq):
    n = x.shape[0]
    tiles_per_seq = seq // PROJ_TM
    cpt = PROJ_TM // LANES
    nblk = seq // LANES
    nt_rows = D_A + D_B + _GATE_ROWS
    tok = lambda i: (i, 0)
    tr = lambda i: (i // tiles_per_seq, i % tiles_per_seq, 0, 0)
    nat = lambda dt: jax.ShapeDtypeStruct((n, D_A), dt)
    return pl.pallas_call(
        _ab_proj_kernel,
        grid=(n // PROJ_TM,),
        in_specs=[
            pl.BlockSpec((PROJ_TM, D_MODEL), tok),
            pl.BlockSpec((None, D_MODEL, _NN_GROUPS * D_A), lambda i: (e, 0, 0)),
            pl.BlockSpec((None, nt_rows, D_MODEL), lambda i: (e, 0, 0)),
            pl.BlockSpec((None, _GATE_ROWS, 1), lambda i: (e, 0, 0)),
        ],
        out_specs=[
            pl.BlockSpec((PROJ_TM, D_A), tok),
            pl.BlockSpec((PROJ_TM, D_A), tok),
            pl.BlockSpec((PROJ_TM, D_A), tok),
            pl.BlockSpec((PROJ_TM, D_A), tok),
            pl.BlockSpec((PROJ_TM, D_A), tok),
            pl.BlockSpec((None, cpt, D_A, LANES), tr),
            pl.BlockSpec((None, cpt, D_B, LANES), tr),
            pl.BlockSpec((None, cpt, _GATE_ROWS, LANES), tr),
        ],
        out_shape=[
            nat(BF16), nat(BF16), nat(BF16), nat(BF16), nat(F32),
            jax.ShapeDtypeStruct((batch, nblk, D_A, LANES), BF16),
            jax.ShapeDtypeStruct((batch, nblk, D_B, LANES), BF16),
            jax.ShapeDtypeStruct((batch, nblk, _GATE_ROWS, LANES), F32),
        ],
        compiler_params=_params("parallel"),
        name="ab_proj",
    )(x, wnn, wnt, gate_b)


def _na_block_window_start(r0, rows):
    return np.clip(r0 - NA_WIN_H // 2, 0, rows - NA_KROWS)


def _na_block_classes(rows):
    reps = []
    cls_of_block = []
    seen = {}
    wh = min(NA_WIN_H, rows)
    for blk in range(rows // NA_ROWS_PER_BLOCK):
        r0 = blk * NA_ROWS_PER_BLOCK
        ws = _na_block_window_start(r0, rows)
        key = tuple(int(np.clip(r0 + d - wh // 2, 0, rows - wh)) - int(ws) for d in range(NA_ROWS_PER_BLOCK)) \
            + tuple(r0 + d - int(ws) for d in range(NA_ROWS_PER_BLOCK))
        if key not in seen:
            seen[key] = len(reps)
            reps.append(r0)
        cls_of_block.append(seen[key])
    return reps, cls_of_block


def _na_bias_tables(rpb, rows):
    reps, _ = _na_block_classes(rows)
    wh = min(NA_WIN_H, rows)
    dr_idx = np.zeros((len(reps), NA_Q, NA_K), np.int32)
    dc_idx = np.zeros((len(reps), NA_Q, NA_K), np.int32)
    valid = np.zeros((len(reps), NA_Q, NA_K), bool)
    qi = np.arange(NA_Q)
    ki = np.arange(NA_K)
    for c, r0 in enumerate(reps):
        ws = int(_na_block_window_start(r0, rows))
        qr = r0 + qi // GRID_W
        qc = qi % GRID_W
        kr = ws + ki // GRID_W
        kc = ki % GRID_W
        rs = np.clip(qr - wh // 2, 0, rows - wh)
        cs = np.clip(qc - NA_WIN_W // 2, 0, GRID_W - NA_WIN_W)
        ok_r = (kr[None, :] >= rs[:, None]) & (kr[None, :] < rs[:, None] + wh)
        ok_c = (kc[None, :] >= cs[:, None]) & (kc[None, :] < cs[:, None] + NA_WIN_W)
        valid[c] = ok_r & ok_c
        dr_idx[c] = np.clip(kr[None, :] - qr[:, None] + (NA_WIN_H - 1), 0, 2 * NA_WIN_H - 2)
        dc_idx[c] = np.clip(kc[None, :] - qc[:, None] + (NA_WIN_W - 1), 0, 2 * NA_WIN_W - 2)
    bias = rpb[:, dr_idx, dc_idx]
    return jnp.where(valid[None], bias, NA_MASKED).astype(F32)


def _na_kernel(q_ref, kt_ref, v_ref, bias_ref, o_ref, *, rows, cls_of_block):
    lane = lax.broadcasted_iota(jnp.int32, (NA_Q, LANES), 1)
    first_head = lane < NA_HEAD_DIM
    kblocks = NA_K // LANES

    def body(blk, carry):
        r0 = blk * NA_ROWS_PER_BLOCK
        ws = jnp.clip(r0 - NA_WIN_H // 2, 0, rows - NA_KROWS)
        cls = jnp.int32(cls_of_block[-1])
        for b in range(len(cls_of_block) - 2, -1, -1):
            cls = jnp.where(blk <= b, jnp.int32(cls_of_block[b]), cls)
        q0 = pl.multiple_of(blk * NA_Q, NA_Q)
        kb0 = ws * GRID_W // LANES
        k0 = pl.multiple_of(kb0 * LANES, LANES)
        q2 = q_ref[pl.ds(q0, NA_Q), :]
        vwin = v_ref[pl.ds(k0, NA_K), :]
        outs = []
        for h in range(2):
            qh = jnp.where(first_head if h == 0 else jnp.logical_not(first_head), q2, jnp.zeros_like(q2))
            s = jnp.concatenate([_dot(qh, kt_ref[kb0 + c]) for c in range(kblocks)], axis=1)
            s = s + bias_ref[h, cls]
            m = jnp.max(s, axis=-1, keepdims=True)
            p = jnp.exp(s - m)
            l = jnp.sum(p, axis=-1, keepdims=True)
            outs.append(_dot(p.astype(BF16), vwin) / l)
        o_ref[pl.ds(q0, NA_Q), :] = jnp.where(first_head, outs[0], outs[1]).astype(BF16)
        return carry

    lax.fori_loop(0, rows // NA_ROWS_PER_BLOCK, body, 0)


def _na_attention(qa, kat, va, bias, batch, seq):
    n = qa.shape[0]
    rows = seq // GRID_W
    _, cls_of_block = _na_block_classes(rows)
    ncls = bias.shape[2]
    pairs = NA_HEADS // 2
    return pl.pallas_call(
        functools.partial(_na_kernel, rows=rows, cls_of_block=tuple(cls_of_block)),
        grid=(pairs, batch),
        in_specs=[
            pl.BlockSpec((seq, LANES), lambda p, b: (b, p)),
            pl.BlockSpec((None, seq // LANES, LANES, LANES), lambda p, b: (b, 0, p, 0)),
            pl.BlockSpec((seq, LANES), lambda p, b: (b, p)),
            pl.BlockSpec((None, 2, ncls, NA_Q, NA_K), lambda p, b: (p, 0, 0, 0, 0)),
        ],
        out_specs=pl.BlockSpec((seq, LANES), lambda p, b: (b, p)),
        out_shape=jax.ShapeDtypeStruct((n, D_A), BF16),
        compiler_params=_params("parallel", "parallel"),
        name="na_attention",
    )(qa, kat, va, bias)


def _log_sigmoid(x):
    return jnp.minimum(x, 0.0) - jnp.log1p(jnp.exp(-jnp.abs(x)))


def _mlstm_kernel(q_ref, kt_ref, v_ref, ob_ref, gt_ref, gn_ref, o_ref, hf_ref, *, seq):
    L = ML_CHUNK
    nc = seq // L
    cb = L // LANES
    scale = ML_HEAD_DIM ** -0.5
    row_i = lax.broadcasted_iota(jnp.int32, (L, L), 0)
    col_i = lax.broadcasted_iota(jnp.int32, (L, L), 1)
    eye = row_i == col_i
    ones_col = (lax.broadcasted_iota(jnp.int32, (L, LANES), 1) == 0).astype(BF16)

    def run(i_row_idx, f_row_idx, reverse, emit):
        tri = (col_i >= row_i) if reverse else (col_i <= row_i)

        def body(ci, carry):
            c_ext, m = carry
            c = (nc - 1 - ci) if reverse else ci
            t0 = pl.multiple_of(c * L, L)
            gates = [gt_ref[c * cb + j] for j in range(cb)]
            i_row = jnp.concatenate([g[i_row_idx:i_row_idx + 1, :] for g in gates], axis=1)
            f_row = jnp.concatenate([g[f_row_idx:f_row_idx + 1, :] for g in gates], axis=1)
            logf = _log_sigmoid(f_row)
            b_col = jnp.sum(jnp.where(tri, logf, 0.0), axis=1, keepdims=True)
            b_row = jnp.sum(jnp.where(eye, b_col, 0.0), axis=0, keepdims=True)
            g_tot = jnp.sum(logf, axis=1, keepdims=True)
            dmat = jnp.where(tri, b_col - b_row + i_row, -jnp.inf)
            inter = b_col + m
            m_t = jnp.maximum(inter, jnp.max(dmat, axis=1, keepdims=True))
            w = jnp.exp(dmat - m_t)
            s_inter = jnp.exp(inter - m_t)
            q = q_ref[pl.ds(t0, L), :]
            kt = jnp.concatenate([kt_ref[c * cb + j] for j in range(cb)], axis=1)
            v_ext = jnp.concatenate([v_ref[pl.ds(t0, L), :], ones_col], axis=1)
            qk = _dot(q, kt) * (w * scale)
            tot = _dot(qk.astype(BF16), v_ext) + s_inter * _dot(q, c_ext.astype(BF16))
            num = tot[:, :ML_HEAD_DIM]
            den = tot[:, ML_HEAD_DIM:ML_HEAD_DIM + 1]
            emit(t0, num / jnp.maximum(jnp.abs(den), jnp.exp(-m_t)))
            a_row = g_tot - b_row + i_row
            m_new = jnp.maximum(g_tot + m, jnp.max(a_row, axis=1, keepdims=True))
            decay = jnp.exp(g_tot + m - m_new)
            wa = jnp.exp(a_row - m_new) * scale
            kv = _dot((kt.astype(F32) * wa).astype(BF16), v_ext)
            return decay * c_ext + kv, m_new

        lax.fori_loop(0, nc, body, (jnp.zeros((ML_HEAD_DIM, 2 * LANES), F32), jnp.zeros((1, 1), F32)))

    def emit_fwd(t0, h):
        hf_ref[pl.ds(t0, L), :] = h

    def emit_bwd(t0, h):
        h = hf_ref[pl.ds(t0, L), :] + h
        mu = jnp.mean(h, axis=-1, keepdims=True)
        hc = h - mu
        var = jnp.mean(hc * hc, axis=-1, keepdims=True)
        hn = hc * lax.rsqrt(var + LN_EPS)
        gate = jax.nn.sigmoid(ob_ref[pl.ds(t0, L), :])
        o_ref[pl.ds(t0, L), :] = (gate * hn * gn_ref[...]).astype(BF16)

    run(0, 1, False, emit_fwd)
    run(2, 3, True, emit_bwd)


def _mlstm(qb, kbt, vb, ob, gt, gn_g, e, batch, seq):
    n = qb.shape[0]
    tok = lambda b, h: (b, h)
    return pl.pallas_call(
        functools.partial(_mlstm_kernel, seq=seq),
        grid=(batch, ML_HEADS),
        in_specs=[
            pl.BlockSpec((seq, LANES), tok),
            pl.BlockSpec((None, seq // LANES, ML_HEAD_DIM, LANES), lambda b, h: (b, 0, h, 0)),
            pl.BlockSpec((seq, LANES), tok),
            pl.BlockSpec((seq, LANES), tok),
            pl.BlockSpec((None, seq // LANES, SUBLANES, LANES), lambda b, h: (b, 0, h, 0)),
            pl.BlockSpec((None, 1, ML_HEAD_DIM), lambda b, h: (e * ML_HEADS + h, 0, 0)),
        ],
        out_specs=pl.BlockSpec((seq, LANES), tok),
        out_shape=jax.ShapeDtypeStruct((n, D_B), BF16),
        scratch_shapes=[pltpu.VMEM((seq, ML_HEAD_DIM), F32)],
        compiler_params=_params("parallel", "parallel"),
        name="mlstm",
    )(qb, kbt, vb, ob, gt, gn_g)


def _ab_out_kernel(x_ref, ya_ref, yb_ref, w_ref, g_ref, b_ref, o_ref):
    mix = _dot(ya_ref[...], w_ref[:D_A, :]) + _dot(yb_ref[...], w_ref[D_A:, :])
    o_ref[...] = _layer_norm(DEEPNORM_ALPHA * x_ref[...] + mix, g_ref[...], b_ref[...])


def _ab_out_ln(x, ya, yb, w_out, ln_g, ln_b, e, ln_idx):
    n = x.shape[0]
    tok = lambda i: (i, 0)
    return pl.pallas_call(
        _ab_out_kernel,
        grid=(n // PROJ_TM,),
        in_specs=[
            pl.BlockSpec((PROJ_TM, D_MODEL), tok),
            pl.BlockSpec((PROJ_TM, D_A), tok),
            pl.BlockSpec((PROJ_TM, D_B), tok),
            pl.BlockSpec((None, D_A + D_B, D_MODEL), lambda i: (e, 0, 0)),
            pl.BlockSpec((None, 1, D_MODEL), lambda i: (ln_idx, 0, 0)),
            pl.BlockSpec((None, 1, D_MODEL), lambda i: (ln_idx, 0, 0)),
        ],
        out_specs=pl.BlockSpec((PROJ_TM, D_MODEL), tok),
        out_shape=jax.ShapeDtypeStruct((n, D_MODEL), F32),
        compiler_params=_params("parallel"),
        name="ab_out_ln",
    )(x, ya, yb, w_out, ln_g, ln_b)


def _conv_kernel(x_ref, xp_ref, xn_ref, win_ref, cw_ref, cb_ref, wout_ref, g_ref, b_ref, o_ref,
                 *, tiles_per_seq):
    i = pl.program_id(0)
    D = D_MODEL
    x = x_ref[...]
    xb = x.astype(BF16)
    u = _dot(xb, win_ref[:, D:2 * D]) * _dot(xb, win_ref[:, 2 * D:])

    def edge_u(ref):
        eb = ref[...].astype(BF16)
        return _dot(eb, win_ref[:, D:2 * D]) * _dot(eb, win_ref[:, 2 * D:])

    pos = i % tiles_per_seq
    u_before = jnp.where(pos == 0, 0.0, edge_u(xp_ref)[SUBLANES - 1:SUBLANES, :])
    u_after = jnp.where(pos == tiles_per_seq - 1, 0.0, edge_u(xn_ref)[0:1, :])
    row = lax.broadcasted_iota(jnp.int32, (PROJ_TM, 1), 0)
    u_m1 = jnp.where(row == 0, u_before, pltpu.roll(u, 1, axis=0))
    u_p1 = jnp.where(row == PROJ_TM - 1, u_after, pltpu.roll(u, PROJ_TM - 1, axis=0))
    y = cw_ref[0:1, :] * u_m1 + cw_ref[1:2, :] * u + cw_ref[2:3, :] * u_p1 + cb_ref[...]
    z = (_dot(xb, win_ref[:, :D]) * y).astype(BF16)
    mix = _dot(z, wout_ref[...])
    o_ref[...] = _layer_norm(DEEPNORM_ALPHA * x + mix, g_ref[...], b_ref[...])


def _conv_mixer_ln(x, w_in, conv_w, conv_b, w_out, ln_g, ln_b, o, ln_idx, seq):
    n = x.shape[0]
    tiles_per_seq = seq // PROJ_TM
    rb = PROJ_TM // SUBLANES
    last_rb = n // SUBLANES - 1
    tok = lambda i: (i, 0)
    return pl.pallas_call(
        functools.partial(_conv_kernel, tiles_per_seq=tiles_per_seq),
        grid=(n // PROJ_TM,),
        in_specs=[
            pl.BlockSpec((PROJ_TM, D_MODEL), tok),
            pl.BlockSpec((SUBLANES, D_MODEL), lambda i: (jnp.maximum(i * rb - 1, 0), 0)),
            pl.BlockSpec((SUBLANES, D_MODEL), lambda i: (jnp.minimum((i + 1) * rb, last_rb), 0)),
            pl.BlockSpec((None, D_MODEL, 3 * D_MODEL), lambda i: (o, 0, 0)),
            pl.BlockSpec((None, 3, D_MODEL), lambda i: (o, 0, 0)),
            pl.BlockSpec((None, 1, D_MODEL), lambda i: (o, 0, 0)),
            pl.BlockSpec((None, D_MODEL, D_MODEL), lambda i: (o, 0, 0)),
            pl.BlockSpec((None, 1, D_MODEL), lambda i: (ln_idx, 0, 0)),
            pl.BlockSpec((None, 1, D_MODEL), lambda i: (ln_idx, 0, 0)),
        ],
        out_specs=pl.BlockSpec((PROJ_TM, D_MODEL), tok),
        out_shape=jax.ShapeDtypeStruct((n, D_MODEL), F32),
        compiler_params=_params("parallel"),
        name="conv_mixer_ln",
    )(x, x, x, w_in, conv_w, conv_b, w_out, ln_g, ln_b)


def _prep_ab_weights(ab_w_in, ab_gate_b):
    n_even = ab_w_in.shape[0]
    w = ab_w_in
    cut = lambda lo, hi: w[:, :, lo:hi]
    qa, ka, va = cut(0, D_A), cut(D_A, 2 * D_A), cut(2 * D_A, 3 * D_A)
    o0 = 3 * D_A
    qb, kb, vb, ob = (cut(o0, o0 + D_B), cut(o0 + D_B, o0 + 2 * D_B),
                      cut(o0 + 2 * D_B, o0 + 3 * D_B), cut(o0 + 3 * D_B, o0 + 4 * D_B))
    gates = cut(o0 + 4 * D_B, o0 + 4 * D_B + 4 * ML_HEADS)
    gates = gates.reshape(n_even, D_MODEL, 4, ML_HEADS).transpose(0, 3, 2, 1)
    gates = jnp.pad(gates, ((0, 0), (0, 0), (0, SUBLANES - 4), (0, 0))).reshape(n_even, _GATE_ROWS, D_MODEL)
    wnn = jnp.concatenate([qa, va, qb, vb, ob], axis=2).astype(BF16)
    wnt = jnp.concatenate([ka.transpose(0, 2, 1), kb.transpose(0, 2, 1), gates], axis=1).astype(BF16)
    gb = jnp.pad(ab_gate_b.astype(F32).transpose(0, 2, 1), ((0, 0), (0, 0), (0, SUBLANES - 4)))
    return wnn, wnt, gb.reshape(n_even, _GATE_ROWS, 1)


def kernel(x, ln_g, ln_b, ffn_w_in, ffn_w_out, ab_w_in, ab_gate_b, na_rpb, ml_gn_g, ab_w_out,
           sc_w_in, sc_conv_w, sc_conv_b, sc_w_out):
    batch, seq, d = x.shape
    rows = seq // GRID_W
    h = x.reshape(batch * seq, d)
    ffn_in = ffn_w_in.astype(BF16)
    ffn_out = ffn_w_out.astype(BF16)
    lng = ln_g.astype(F32).reshape(DEPTH * 3, 1, d)
    lnb = ln_b.astype(F32).reshape(DEPTH * 3, 1, d)
    wnn, wnt, gate_b = _prep_ab_weights(ab_w_in, ab_gate_b)
    ab_out = ab_w_out.astype(BF16)
    gn = ml_gn_g.astype(F32).reshape(-1, 1, ML_HEAD_DIM)
    sc_in = sc_w_in.astype(BF16)
    sc_out = sc_w_out.astype(BF16)
    conv_b = sc_conv_b.astype(F32).reshape(-1, 1, d)
    conv_w = sc_conv_w.astype(F32)

    for layer in range(DEPTH):
        h = _ffn_ln(h, ffn_in, ffn_out, lng, lnb, layer, 0, layer * 3)
        if layer % 2 == 0:
            e = layer // 2
            bias = _na_bias_tables(na_rpb[e].astype(F32), rows)
            bias = bias.reshape(NA_HEADS // 2, 2, *bias.shape[1:])
            qa, va, qb, vb, ob, kat, kbt, gt = _ab_proj(h, wnn, wnt, gate_b, e, batch, seq)
            ya = _na_attention(qa, kat, va, bias, batch, seq)
            yb = _mlstm(qb, kbt, vb, ob, gt, gn, e, batch, seq)
            h = _ab_out_ln(h, ya, yb, ab_out, lng, lnb, e, layer * 3 + 1)
        else:
            o = layer // 2
            h = _conv_mixer_ln(h, sc_in, conv_w, conv_b, sc_out, lng, lnb, o, layer * 3 + 1, seq)
        h = _ffn_ln(h, ffn_in, ffn_out, lng, lnb, layer, 1, layer * 3 + 2)
    return h.reshape(batch, seq, d)
```

```python
import functools

import numpy as np
import jax
import jax.numpy as jnp
from jax import lax
from jax.experimental import pallas as pl
from jax.experimental.pallas import tpu as pltpu

D_MODEL = 1024
DEPTH = 4
GRID_W = 64
NA_HEAD_DIM = 64
NA_HEADS = 8
NA_WIN_H = 8
NA_WIN_W = 16
ML_HEADS = 4
ML_HEAD_DIM = 128
D_A = NA_HEADS * NA_HEAD_DIM
D_B = ML_HEADS * ML_HEAD_DIM
D_FF = 2816
DEEPNORM_ALPHA = (2 * DEPTH) ** 0.25
LN_EPS = 1e-5

LANES = 128
SUBLANES = 8
VMEM_LIMIT = 52 * 1024 * 1024

FFN_TM = 512
FFN_TF = 256
PROJ_TM = 512
ML_CHUNK = 256
NA_ROWS_PER_BLOCK = 2
NA_Q = NA_ROWS_PER_BLOCK * GRID_W
NA_KROWS = 10
NA_K = NA_KROWS * GRID_W
NA_MASKED = -1e30
NA_UNROLL = 2

BF16 = jnp.bfloat16
F32 = jnp.float32


def _dot(a, b):
    return jnp.dot(a, b, preferred_element_type=F32)


def _layer_norm(y, g, b):
    mu = jnp.mean(y, axis=-1, keepdims=True)
    yc = y - mu
    var = jnp.mean(yc * yc, axis=-1, keepdims=True)
    return yc * lax.rsqrt(var + LN_EPS) * g + b


def _params(*sem):
    return pltpu.CompilerParams(dimension_semantics=sem, vmem_limit_bytes=VMEM_LIMIT)


def _ffn_kernel(x_ref, wg_ref, wu_ref, wo_ref, g_ref, b_ref, o_ref):
    x = x_ref[...]
    xb = x.astype(BF16)
    acc = None
    for c in range(D_FF // FFN_TF):
        cols = slice(c * FFN_TF, (c + 1) * FFN_TF)
        gate = _dot(xb, wg_ref[:, cols])
        up = _dot(xb, wu_ref[:, cols])
        act = (gate * jax.nn.sigmoid(gate) * up).astype(BF16)
        part = _dot(act, wo_ref[cols, :])
        acc = part if acc is None else acc + part
    o_ref[...] = _layer_norm(DEEPNORM_ALPHA * x + 0.5 * acc, g_ref[...], b_ref[...])


def _resident(block_shape, index_map):
    return pl.BlockSpec(block_shape, index_map, pipeline_mode=pl.Buffered(1))


def _ffn_ln(x, w_in, w_out, ln_g, ln_b, layer, k, ln_idx):
    n = x.shape[0]
    return pl.pallas_call(
        _ffn_kernel,
        grid=(n // FFN_TM,),
        in_specs=[
            pl.BlockSpec((FFN_TM, D_MODEL), lambda i: (i, 0)),
            _resident((None, None, D_MODEL, D_FF), lambda i: (layer, k, 0, 0)),
            _resident((None, None, D_MODEL, D_FF), lambda i: (layer, k, 0, 1)),
            _resident((None, None, D_FF, D_MODEL), lambda i: (layer, k, 0, 0)),
            pl.BlockSpec((None, 1, D_MODEL), lambda i: (ln_idx, 0, 0)),
            pl.BlockSpec((None, 1, D_MODEL), lambda i: (ln_idx, 0, 0)),
        ],
        out_specs=pl.BlockSpec((FFN_TM, D_MODEL), lambda i: (i, 0)),
        out_shape=jax.ShapeDtypeStruct((n, D_MODEL), F32),
        compiler_params=_params("parallel"),
        name="ffn_ln",
    )(x, w_in, w_in, w_out, ln_g, ln_b)


_NN_GROUPS = 5
_GATE_ROWS = ML_HEADS * SUBLANES


def _ab_proj_kernel(x_ref, wnn_ref, wnt_ref, gb_ref,
                    qa_ref, va_ref, qb_ref, vb_ref, ob_ref, kat_ref, kbt_ref, gt_ref):
    xb = x_ref[...].astype(BF16)

    def nn(g):
        return _dot(xb, wnn_ref[:, g * D_A:(g + 1) * D_A])

    def nt(lo, hi):
        return lax.dot_general(wnt_ref[lo:hi, :], xb, (((1,), (1,)), ((), ())),
                               preferred_element_type=F32)

    qa_ref[...] = (nn(0) * (NA_HEAD_DIM ** -0.5)).astype(BF16)
    va_ref[...] = nn(1).astype(BF16)
    qb_ref[...] = nn(2).astype(BF16)
    vb_ref[...] = nn(3).astype(BF16)
    ob_ref[...] = nn(4)
    kat = nt(0, D_A).astype(BF16)
    kbt = nt(D_A, D_A + D_B).astype(BF16)
    gt = nt(D_A + D_B, D_A + D_B + _GATE_ROWS) + gb_ref[...]
    for c in range(PROJ_TM // LANES):
        kat_ref[c] = kat[:, c * LANES:(c + 1) * LANES]
        kbt_ref[c] = kbt[:, c * LANES:(c + 1) * LANES]
        gt_ref[c] = gt[:, c * LANES:(c + 1) * LANES]


def _ab_proj(x, wnn, wnt, gate_b, e, batch, seq):
    n = x.shape[0]
    tiles_per_seq = seq // PROJ_TM
    cpt = PROJ_TM // LANES
    nblk = seq // LANES
    nt_rows = D_A + D_B + _GATE_ROWS
    tok = lambda i: (i, 0)
    tr = lambda i: (i // tiles_per_seq, i % tiles_per_seq, 0, 0)
    nat = lambda dt: jax.ShapeDtypeStruct((n, D_A), dt)
    return pl.pallas_call(
        _ab_proj_kernel,
        grid=(n // PROJ_TM,),
        in_specs=[
            pl.BlockSpec((PROJ_TM, D_MODEL), tok),
            pl.BlockSpec((None, D_MODEL, _NN_GROUPS * D_A), lambda i: (e, 0, 0)),
            pl.BlockSpec((None, nt_rows, D_MODEL), lambda i: (e, 0, 0)),
            pl.BlockSpec((None, _GATE_ROWS, 1), lambda i: (e, 0, 0)),
        ],
        out_specs=[
            pl.BlockSpec((PROJ_TM, D_A), tok),
            pl.BlockSpec((PROJ_TM, D_A), tok),
            pl.BlockSpec((PROJ_TM, D_A), tok),
            pl.BlockSpec((PROJ_TM, D_A), tok),
            pl.BlockSpec((PROJ_TM, D_A), tok),
            pl.BlockSpec((None, cpt, D_A, LANES), tr),
            pl.BlockSpec((None, cpt, D_B, LANES), tr),
            pl.BlockSpec((None, cpt, _GATE_ROWS, LANES), tr),
        ],
        out_shape=[
            nat(BF16), nat(BF16), nat(BF16), nat(BF16), nat(F32),
            jax.ShapeDtypeStruct((batch, nblk, D_A, LANES), BF16),
            jax.ShapeDtypeStruct((batch, nblk, D_B, LANES), BF16),
            jax.ShapeDtypeStruct((batch, nblk, _GATE_ROWS, LANES), F32),
        ],
        compiler_params=_params("parallel"),
        name="ab_proj",
    )(x, wnn, wnt, gate_b)


def _na_block_window_start(r0, rows):
    return np.clip(r0 - NA_WIN_H // 2, 0, rows - NA_KROWS)


def _na_block_classes(rows):
    reps = []
    cls_of_block = []
    seen = {}
    wh = min(NA_WIN_H, rows)
    for blk in range(rows // NA_ROWS_PER_BLOCK):
        r0 = blk * NA_ROWS_PER_BLOCK
        ws = _na_block_window_start(r0, rows)
        key = tuple(int(np.clip(r0 + d - wh // 2, 0, rows - wh)) - int(ws) for d in range(NA_ROWS_PER_BLOCK)) \
            + tuple(r0 + d - int(ws) for d in range(NA_ROWS_PER_BLOCK))
        if key not in seen:
            seen[key] = len(reps)
            reps.append(r0)
        cls_of_block.append(seen[key])
    return reps, cls_of_block


def _na_bias_tables(rpb, rows):
    reps, _ = _na_block_classes(rows)
    wh = min(NA_WIN_H, rows)
    n_dr, n_dc = 2 * NA_WIN_H - 1, 2 * NA_WIN_W - 1
    col = np.arange(GRID_W)
    dc = np.clip(col[None, :] - col[:, None] + (NA_WIN_W - 1), 0, n_dc - 1)
    cs = np.clip(col - NA_WIN_W // 2, 0, GRID_W - NA_WIN_W)
    ok_c = (col[None, :] >= cs[:, None]) & (col[None, :] < cs[:, None] + NA_WIN_W)
    sel_dc = (dc[None] == np.arange(n_dc)[:, None, None]).astype(np.float32)
    sel_dr = np.zeros((len(reps), NA_ROWS_PER_BLOCK, NA_KROWS, n_dr), np.float32)
    ok_r = np.zeros((len(reps), NA_ROWS_PER_BLOCK, NA_KROWS), bool)
    for c, r0 in enumerate(reps):
        ws = int(_na_block_window_start(r0, rows))
        for dq in range(NA_ROWS_PER_BLOCK):
            qr = r0 + dq
            rs = int(np.clip(qr - wh // 2, 0, rows - wh))
            for i in range(NA_KROWS):
                kr = ws + i
                if rs <= kr < rs + wh:
                    ok_r[c, dq, i] = True
                    sel_dr[c, dq, i, kr - qr + (NA_WIN_H - 1)] = 1.0
    hi = lax.Precision.HIGHEST
    by_col = jnp.einsum('hrd,dqk->hrqk', rpb, sel_dc, precision=hi)
    bias = jnp.einsum('hrqk,cdir->hcdqik', by_col, sel_dr, precision=hi)
    valid = ok_r[:, :, None, :, None] & ok_c[None, None, :, None, :]
    bias = jnp.where(valid[None], bias, NA_MASKED).astype(F32)
    return bias.reshape(rpb.shape[0], len(reps), NA_Q, NA_K)


def _na_kernel(q_ref, kt_ref, v_ref, bias_ref, o_ref, *, rows, cls_of_block):
    lane = lax.broadcasted_iota(jnp.int32, (NA_Q, LANES), 1)
    first_head = lane < NA_HEAD_DIM
    kblocks = NA_K // LANES

    def body(it, carry):
        blocks = []
        for u in range(NA_UNROLL):
            blk = it * NA_UNROLL + u
            r0 = blk * NA_ROWS_PER_BLOCK
            ws = jnp.clip(r0 - NA_WIN_H // 2, 0, rows - NA_KROWS)
            cls = jnp.int32(cls_of_block[-1])
            for b in range(len(cls_of_block) - 2, -1, -1):
                cls = jnp.where(blk <= b, jnp.int32(cls_of_block[b]), cls)
            kb0 = ws * GRID_W // LANES
            blocks.append((pl.multiple_of(blk * NA_Q, NA_Q), kb0, pl.multiple_of(kb0 * LANES, LANES), cls))
        scores = []
        for q0, kb0, _, cls in blocks:
            q2 = q_ref[pl.ds(q0, NA_Q), :]
            for h in range(2):
                qh = jnp.where(first_head if h == 0 else jnp.logical_not(first_head), q2, jnp.zeros_like(q2))
                s = jnp.concatenate([_dot(qh, kt_ref[kb0 + c]) for c in range(kblocks)], axis=1)
                scores.append(s + bias_ref[h, cls])
        probs = []
        for s in scores:
            p = jnp.exp(s - jnp.max(s, axis=-1, keepdims=True))
            probs.append((p.astype(BF16), jnp.sum(p, axis=-1, keepdims=True)))
        for u, (q0, _, k0, _) in enumerate(blocks):
            vwin = v_ref[pl.ds(k0, NA_K), :]
            outs = [_dot(p, vwin) / l for p, l in probs[2 * u:2 * u + 2]]
            o_ref[pl.ds(q0, NA_Q), :] = jnp.where(first_head, outs[0], outs[1]).astype(BF16)
        return carry

    lax.fori_loop(0, rows // (NA_ROWS_PER_BLOCK * NA_UNROLL), body, 0)


def _na_attention(qa, kat, va, bias, batch, seq):
    n = qa.shape[0]
    rows = seq // GRID_W
    _, cls_of_block = _na_block_classes(rows)
    ncls = bias.shape[2]
    pairs = NA_HEADS // 2
    return pl.pallas_call(
        functools.partial(_na_kernel, rows=rows, cls_of_block=tuple(cls_of_block)),
        grid=(pairs, batch),
        in_specs=[
            pl.BlockSpec((seq, LANES), lambda p, b: (b, p)),
            pl.BlockSpec((None, seq // LANES, LANES, LANES), lambda p, b: (b, 0, p, 0)),
            pl.BlockSpec((seq, LANES), lambda p, b: (b, p)),
            pl.BlockSpec((None, 2, ncls, NA_Q, NA_K), lambda p, b: (p, 0, 0, 0, 0)),
        ],
        out_specs=pl.BlockSpec((seq, LANES), lambda p, b: (b, p)),
        out_shape=jax.ShapeDtypeStruct((n, D_A), BF16),
        compiler_params=_params("parallel", "parallel"),
        name="na_attention",
    )(qa, kat, va, bias)


def _log_sigmoid(x):
    return jnp.minimum(x, 0.0) - jnp.log1p(jnp.exp(-jnp.abs(x)))


def _mlstm_kernel(q_ref, kt_ref, v_ref, ob_ref, gt_ref, gn_ref, o_ref, h_ref, *, seq):
    L = ML_CHUNK
    nc = seq // L
    assert nc % 2 == 0
    cb = L // LANES
    scale = ML_HEAD_DIM ** -0.5
    row_i = lax.broadcasted_iota(jnp.int32, (L, L), 0)
    col_i = lax.broadcasted_iota(jnp.int32, (L, L), 1)
    eye = row_i == col_i
    ones_col = (lax.broadcasted_iota(jnp.int32, (L, LANES), 1) == 0).astype(BF16)

    def stage_load(c, carry, i_row_idx, f_row_idx, reverse):
        t0 = pl.multiple_of(c * L, L)
        gates = [gt_ref[c * cb + j] for j in range(cb)]
        st = dict(
            tri=(col_i >= row_i) if reverse else (col_i <= row_i),
            i_row=jnp.concatenate([g[i_row_idx:i_row_idx + 1, :] for g in gates], axis=1),
            f_row=jnp.concatenate([g[f_row_idx:f_row_idx + 1, :] for g in gates], axis=1),
            q=q_ref[pl.ds(t0, L), :],
            kt=jnp.concatenate([kt_ref[c * cb + j] for j in range(cb)], axis=1),
            v_ext=jnp.concatenate([v_ref[pl.ds(t0, L), :], ones_col], axis=1),
            c_ext=carry[0], m=carry[1])
        st['qk'] = _dot(st['q'], st['kt'])
        st['qc'] = _dot(st['q'], st['c_ext'].astype(BF16))
        return st

    def stage_gates(st):
        tri, i_row, m = st['tri'], st['i_row'], st['m']
        logf = _log_sigmoid(st['f_row'])
        b_col = jnp.sum(jnp.where(tri, logf, 0.0), axis=1, keepdims=True)
        b_row = jnp.sum(jnp.where(eye, b_col, 0.0), axis=0, keepdims=True)
        g_tot = jnp.sum(logf, axis=1, keepdims=True)
        a_row = g_tot - b_row + i_row
        m_new = jnp.maximum(g_tot + m, jnp.max(a_row, axis=1, keepdims=True))
        st['m_new'] = m_new
        st['decay'] = jnp.exp(g_tot + m - m_new)
        st['wa'] = jnp.exp(a_row - m_new) * scale
        dmat = jnp.where(tri, b_col - b_row + i_row, -jnp.inf)
        inter = b_col + m
        m_t = jnp.maximum(inter, jnp.max(dmat, axis=1, keepdims=True))
        st['w'] = jnp.exp(dmat - m_t) * scale
        st['s_inter'] = jnp.exp(inter - m_t)
        st['floor'] = jnp.exp(-m_t)

    def stage_state(st):
        kv = _dot((st['kt'].astype(F32) * st['wa']).astype(BF16), st['v_ext'])
        return st['decay'] * st['c_ext'] + kv, st['m_new']

    def stage_out(st):
        tot = _dot((st['qk'] * st['w']).astype(BF16), st['v_ext']) + st['s_inter'] * st['qc']
        num = tot[:, :ML_HEAD_DIM]
        den = tot[:, ML_HEAD_DIM:ML_HEAD_DIM + 1]
        return num / jnp.maximum(jnp.abs(den), st['floor'])

    def finish(c, h):
        rows = pl.ds(pl.multiple_of(c * L, L), L)
        mu = jnp.mean(h, axis=-1, keepdims=True)
        hc = h - mu
        var = jnp.mean(hc * hc, axis=-1, keepdims=True)
        hn = hc * lax.rsqrt(var + LN_EPS)
        gate = jax.nn.sigmoid(ob_ref[rows, :])
        o_ref[rows, :] = (gate * hn * gn_ref[...]).astype(BF16)

    def body(ci, carry, second_half):
        cf, cr = ci, nc - 1 - ci
        sf = stage_load(cf, carry[0], 0, 1, False)
        sr = stage_load(cr, carry[1], 2, 3, True)
        stage_gates(sf)
        stage_gates(sr)
        fwd = stage_state(sf)
        bwd = stage_state(sr)
        h_f = stage_out(sf)
        h_r = stage_out(sr)
        rows_f = pl.ds(pl.multiple_of(cf * L, L), L)
        rows_r = pl.ds(pl.multiple_of(cr * L, L), L)
        if second_half:
            finish(cf, h_f + h_ref[rows_f, :])
            finish(cr, h_r + h_ref[rows_r, :])
        else:
            h_ref[rows_f, :] = h_f
            h_ref[rows_r, :] = h_r
        return fwd, bwd

    zero = (jnp.zeros((ML_HEAD_DIM, 2 * LANES), F32), jnp.zeros((1, 1), F32))
    carry = lax.fori_loop(0, nc // 2, functools.partial(body, second_half=False), (zero, zero))
    lax.fori_loop(nc // 2, nc, functools.partial(body, second_half=True), carry)


def _mlstm(qb, kbt, vb, ob, gt, gn_g, e, batch, seq):
    n = qb.shape[0]
    tok = lambda b, h: (b, h)
    return pl.pallas_call(
        functools.partial(_mlstm_kernel, seq=seq),
        grid=(batch, ML_HEADS),
        in_specs=[
            pl.BlockSpec((seq, LANES), tok),
            pl.BlockSpec((None, seq // LANES, ML_HEAD_DIM, LANES), lambda b, h: (b, 0, h, 0)),
            pl.BlockSpec((seq, LANES), tok),
            pl.BlockSpec((seq, LANES), tok),
            pl.BlockSpec((None, seq // LANES, SUBLANES, LANES), lambda b, h: (b, 0, h, 0)),
            pl.BlockSpec((None, 1, ML_HEAD_DIM), lambda b, h: (e * ML_HEADS + h, 0, 0)),
        ],
        out_specs=pl.BlockSpec((seq, LANES), tok),
        out_shape=jax.ShapeDtypeStruct((n, D_B), BF16),
        scratch_shapes=[pltpu.VMEM((seq, ML_HEAD_DIM), F32)],
        compiler_params=_params("parallel", "parallel"),
        name="mlstm",
    )(qb, kbt, vb, ob, gt, gn_g)


def _ab_out_kernel(x_ref, ya_ref, yb_ref, w_ref, g_ref, b_ref, o_ref):
    mix = _dot(ya_ref[...], w_ref[:D_A, :]) + _dot(yb_ref[...], w_ref[D_A:, :])
    o_ref[...] = _layer_norm(DEEPNORM_ALPHA * x_ref[...] + mix, g_ref[...], b_ref[...])


def _ab_out_ln(x, ya, yb, w_out, ln_g, ln_b, e, ln_idx):
    n = x.shape[0]
    tok = lambda i: (i, 0)
    return pl.pallas_call(
        _ab_out_kernel,
        grid=(n // PROJ_TM,),
        in_specs=[
            pl.BlockSpec((PROJ_TM, D_MODEL), tok),
            pl.BlockSpec((PROJ_TM, D_A), tok),
            pl.BlockSpec((PROJ_TM, D_B), tok),
            pl.BlockSpec((None, D_A + D_B, D_MODEL), lambda i: (e, 0, 0)),
            pl.BlockSpec((None, 1, D_MODEL), lambda i: (ln_idx, 0, 0)),
            pl.BlockSpec((None, 1, D_MODEL), lambda i: (ln_idx, 0, 0)),
        ],
        out_specs=pl.BlockSpec((PROJ_TM, D_MODEL), tok),
        out_shape=jax.ShapeDtypeStruct((n, D_MODEL), F32),
        compiler_params=_params("parallel"),
        name="ab_out_ln",
    )(x, ya, yb, w_out, ln_g, ln_b)


def _conv_kernel(x_ref, xp_ref, xn_ref, win_ref, cw_ref, cb_ref, wout_ref, g_ref, b_ref, o_ref,
                 *, tiles_per_seq):
    i = pl.program_id(0)
    D = D_MODEL
    x = x_ref[...]
    xb = x.astype(BF16)
    u = _dot(xb, win_ref[:, D:2 * D]) * _dot(xb, win_ref[:, 2 * D:])

    def edge_u(ref):
        eb = ref[...].astype(BF16)
        return _dot(eb, win_ref[:, D:2 * D]) * _dot(eb, win_ref[:, 2 * D:])

    pos = i % tiles_per_seq
    u_before = jnp.where(pos == 0, 0.0, edge_u(xp_ref)[SUBLANES - 1:SUBLANES, :])
    u_after = jnp.where(pos == tiles_per_seq - 1, 0.0, edge_u(xn_ref)[0:1, :])
    row = lax.broadcasted_iota(jnp.int32, (PROJ_TM, 1), 0)
    u_m1 = jnp.where(row == 0, u_before, pltpu.roll(u, 1, axis=0))
    u_p1 = jnp.where(row == PROJ_TM - 1, u_after, pltpu.roll(u, PROJ_TM - 1, axis=0))
    y = cw_ref[0:1, :] * u_m1 + cw_ref[1:2, :] * u + cw_ref[2:3, :] * u_p1 + cb_ref[...]
    z = (_dot(xb, win_ref[:, :D]) * y).astype(BF16)
    mix = _dot(z, wout_ref[...])
    o_ref[...] = _layer_norm(DEEPNORM_ALPHA * x + mix, g_ref[...], b_ref[...])


def _conv_mixer_ln(x, w_in, conv_w, conv_b, w_out, ln_g, ln_b, o, ln_idx, seq):
    n = x.shape[0]
    tiles_per_seq = seq // PROJ_TM
    rb = PROJ_TM // SUBLANES
    last_rb = n // SUBLANES - 1
    tok = lambda i: (i, 0)
    return pl.pallas_call(
        functools.partial(_conv_kernel, tiles_per_seq=tiles_per_seq),
        grid=(n // PROJ_TM,),
        in_specs=[
            pl.BlockSpec((PROJ_TM, D_MODEL), tok),
            pl.BlockSpec((SUBLANES, D_MODEL), lambda i: (jnp.maximum(i * rb - 1, 0), 0)),
            pl.BlockSpec((SUBLANES, D_MODEL), lambda i: (jnp.minimum((i + 1) * rb, last_rb), 0)),
            pl.BlockSpec((None, D_MODEL, 3 * D_MODEL), lambda i: (o, 0, 0)),
            pl.BlockSpec((None, 3, D_MODEL), lambda i: (o, 0, 0)),
            pl.BlockSpec((None, 1, D_MODEL), lambda i: (o, 0, 0)),
            pl.BlockSpec((None, D_MODEL, D_MODEL), lambda i: (o, 0, 0)),
            pl.BlockSpec((None, 1, D_MODEL), lambda i: (ln_idx, 0, 0)),
            pl.BlockSpec((None, 1, D_MODEL), lambda i: (ln_idx, 0, 0)),
        ],
        out_specs=pl.BlockSpec((PROJ_TM, D_MODEL), tok),
        out_shape=jax.ShapeDtypeStruct((n, D_MODEL), F32),
        compiler_params=_params("parallel"),
        name="conv_mixer_ln",
    )(x, x, x, w_in, conv_w, conv_b, w_out, ln_g, ln_b)


def _prep_ab_weights(ab_w_in, ab_gate_b):
    n_even = ab_w_in.shape[0]
    w = ab_w_in
    cut = lambda lo, hi: w[:, :, lo:hi]
    qa, ka, va = cut(0, D_A), cut(D_A, 2 * D_A), cut(2 * D_A, 3 * D_A)
    o0 = 3 * D_A
    qb, kb, vb, ob = (cut(o0, o0 + D_B), cut(o0 + D_B, o0 + 2 * D_B),
                      cut(o0 + 2 * D_B, o0 + 3 * D_B), cut(o0 + 3 * D_B, o0 + 4 * D_B))
    gates = cut(o0 + 4 * D_B, o0 + 4 * D_B + 4 * ML_HEADS)
    gates = gates.reshape(n_even, D_MODEL, 4, ML_HEADS).transpose(0, 3, 2, 1)
    gates = jnp.pad(gates, ((0, 0), (0, 0), (0, SUBLANES - 4), (0, 0))).reshape(n_even, _GATE_ROWS, D_MODEL)
    wnn = jnp.concatenate([qa, va, qb, vb, ob], axis=2).astype(BF16)
    wnt = jnp.concatenate([ka.transpose(0, 2, 1), kb.transpose(0, 2, 1), gates], axis=1).astype(BF16)
    gb = jnp.pad(ab_gate_b.astype(F32).transpose(0, 2, 1), ((0, 0), (0, 0), (0, SUBLANES - 4)))
    return wnn, wnt, gb.reshape(n_even, _GATE_ROWS, 1)


def kernel(x, ln_g, ln_b, ffn_w_in, ffn_w_out, ab_w_in, ab_gate_b, na_rpb, ml_gn_g, ab_w_out,
           sc_w_in, sc_conv_w, sc_conv_b, sc_w_out):
    batch, seq, d = x.shape
    rows = seq // GRID_W
    h = x.reshape(batch * seq, d)
    ffn_in = ffn_w_in.astype(BF16)
    ffn_out = ffn_w_out.astype(BF16)
    lng = ln_g.astype(F32).reshape(DEPTH * 3, 1, d)
    lnb = ln_b.astype(F32).reshape(DEPTH * 3, 1, d)
    wnn, wnt, gate_b = _prep_ab_weights(ab_w_in, ab_gate_b)
    ab_out = ab_w_out.astype(BF16)
    gn = ml_gn_g.astype(F32).reshape(-1, 1, ML_HEAD_DIM)
    sc_in = sc_w_in.astype(BF16)
    sc_out = sc_w_out.astype(BF16)
    conv_b = sc_conv_b.astype(F32).reshape(-1, 1, d)
    conv_w = sc_conv_w.astype(F32)

    for layer in range(DEPTH):
        h = _ffn_ln(h, ffn_in, ffn_out, lng, lnb, layer, 0, layer * 3)
        if layer % 2 == 0:
            e = layer // 2
            bias = _na_bias_tables(na_rpb[e].astype(F32), rows)
            bias = bias.reshape(NA_HEADS // 2, 2, *bias.shape[1:])
            qa, va, qb, vb, ob, kat, kbt, gt = _ab_proj(h, wnn, wnt, gate_b, e, batch, seq)
            ya = _na_attention(qa, kat, va, bias, batch, seq)
            yb = _mlstm(qb, kbt, vb, ob, gt, gn, e, batch, seq)
            h = _ab_out_ln(h, ya, yb, ab_out, lng, lnb, e, layer * 3 + 1)
        else:
            o = layer // 2
            h = _conv_mixer_ln(h, sc_in, conv_w, conv_b, sc_out, lng, lnb, o, layer * 3 + 1, seq)
        h = _ffn_ln(h, ffn_in, ffn_out, lng, lnb, layer, 1, layer * 3 + 2)
    return h.reshape(batch, seq, d)
```

```python
import functools

import numpy as np
import jax
import jax.numpy as jnp
from jax import lax
from jax.experimental import pallas as pl
from jax.experimental.pallas import tpu as pltpu

D_MODEL = 1024
DEPTH = 4
GRID_W = 64
NA_HEAD_DIM = 64
NA_HEADS = 8
NA_WIN_H = 8
NA_WIN_W = 16
ML_HEADS = 4
ML_HEAD_DIM = 128
D_A = NA_HEADS * NA_HEAD_DIM
D_B = ML_HEADS * ML_HEAD_DIM
D_FF = 2816
DEEPNORM_ALPHA = (2 * DEPTH) ** 0.25
LN_EPS = 1e-5

LANES = 128
SUBLANES = 8
VMEM_LIMIT = 52 * 1024 * 1024

FFN_TM = 512
FFN_TF = 256
PROJ_TM = 512
ML_CHUNK = 256
ML_HEADS_PER_STEP = 1
NA_ROWS_PER_BLOCK = 2
NA_Q = NA_ROWS_PER_BLOCK * GRID_W
NA_KROWS = 10
NA_K = NA_KROWS * GRID_W
NA_MASKED = -1e30
NA_UNROLL = 2

BF16 = jnp.bfloat16
F32 = jnp.float32


def _dot(a, b):
    return jnp.dot(a, b, preferred_element_type=F32)


def _layer_norm(y, g, b):
    mu = jnp.mean(y, axis=-1, keepdims=True)
    yc = y - mu
    var = jnp.mean(yc * yc, axis=-1, keepdims=True)
    return yc * lax.rsqrt(var + LN_EPS) * g + b


def _params(*sem):
    return pltpu.CompilerParams(dimension_semantics=sem, vmem_limit_bytes=VMEM_LIMIT)


def _ffn_kernel(x_ref, wg_ref, wu_ref, wo_ref, g_ref, b_ref, o_ref):
    x = x_ref[...]
    xb = x.astype(BF16)
    acc = None
    for c in range(D_FF // FFN_TF):
        cols = slice(c * FFN_TF, (c + 1) * FFN_TF)
        gate = _dot(xb, wg_ref[:, cols])
        up = _dot(xb, wu_ref[:, cols])
        act = (gate * jax.nn.sigmoid(gate) * up).astype(BF16)
        part = _dot(act, wo_ref[cols, :])
        acc = part if acc is None else acc + part
    o_ref[...] = _layer_norm(DEEPNORM_ALPHA * x + 0.5 * acc, g_ref[...], b_ref[...])


def _resident(block_shape, index_map):
    return pl.BlockSpec(block_shape, index_map, pipeline_mode=pl.Buffered(1))


def _ffn_ln(x, w_in, w_out, ln_g, ln_b, layer, k, ln_idx):
    n = x.shape[0]
    return pl.pallas_call(
        _ffn_kernel,
        grid=(n // FFN_TM,),
        in_specs=[
            pl.BlockSpec((FFN_TM, D_MODEL), lambda i: (i, 0)),
            _resident((None, None, D_MODEL, D_FF), lambda i: (layer, k, 0, 0)),
            _resident((None, None, D_MODEL, D_FF), lambda i: (layer, k, 0, 1)),
            _resident((None, None, D_FF, D_MODEL), lambda i: (layer, k, 0, 0)),
            pl.BlockSpec((None, 1, D_MODEL), lambda i: (ln_idx, 0, 0)),
            pl.BlockSpec((None, 1, D_MODEL), lambda i: (ln_idx, 0, 0)),
        ],
        out_specs=pl.BlockSpec((FFN_TM, D_MODEL), lambda i: (i, 0)),
        out_shape=jax.ShapeDtypeStruct((n, D_MODEL), F32),
        compiler_params=_params("parallel"),
        name="ffn_ln",
    )(x, w_in, w_in, w_out, ln_g, ln_b)


_NN_GROUPS = 5
_GATE_ROWS = ML_HEADS * SUBLANES


def _ab_proj_kernel(x_ref, wnn_ref, wnt_ref, gb_ref,
                    qa_ref, va_ref, qb_ref, vb_ref, ob_ref, kat_ref, kbt_ref, gt_ref):
    xb = x_ref[...].astype(BF16)

    def nn(g):
        return _dot(xb, wnn_ref[:, g * D_A:(g + 1) * D_A])

    def nt(lo, hi):
        return lax.dot_general(wnt_ref[lo:hi, :], xb, (((1,), (1,)), ((), ())),
                               preferred_element_type=F32)

    qa_ref[...] = (nn(0) * (NA_HEAD_DIM ** -0.5)).astype(BF16)
    va_ref[...] = nn(1).astype(BF16)
    qb_ref[...] = nn(2).astype(BF16)
    vb_ref[...] = nn(3).astype(BF16)
    ob_ref[...] = nn(4)
    kat = nt(0, D_A).astype(BF16)
    kbt = nt(D_A, D_A + D_B).astype(BF16)
    gt = nt(D_A + D_B, D_A + D_B + _GATE_ROWS) + gb_ref[...]
    for c in range(PROJ_TM // LANES):
        kat_ref[c] = kat[:, c * LANES:(c + 1) * LANES]
        kbt_ref[c] = kbt[:, c * LANES:(c + 1) * LANES]
        gt_ref[c] = gt[:, c * LANES:(c + 1) * LANES]


def _ab_proj(x, wnn, wnt, gate_b, e, batch, seq):
    n = x.shape[0]
    tiles_per_seq = seq // PROJ_TM
    cpt = PROJ_TM // LANES
    nblk = seq // LANES
    nt_rows = D_A + D_B + _GATE_ROWS
    tok = lambda i: (i, 0)
    tr = lambda i: (i // tiles_per_seq, i % tiles_per_seq, 0, 0)
    nat = lambda dt: jax.ShapeDtypeStruct((n, D_A), dt)
    return pl.pallas_call(
        _ab_proj_kernel,
        grid=(n // PROJ_TM,),
        in_specs=[
            pl.BlockSpec((PROJ_TM, D_MODEL), tok),
            pl.BlockSpec((None, D_MODEL, _NN_GROUPS * D_A), lambda i: (e, 0, 0)),
            pl.BlockSpec((None, nt_rows, D_MODEL), lambda i: (e, 0, 0)),
            pl.BlockSpec((None, _GATE_ROWS, 1), lambda i: (e, 0, 0)),
        ],
        out_specs=[
            pl.BlockSpec((PROJ_TM, D_A), tok),
            pl.BlockSpec((PROJ_TM, D_A), tok),
            pl.BlockSpec((PROJ_TM, D_A), tok),
            pl.BlockSpec((PROJ_TM, D_A), tok),
            pl.BlockSpec((PROJ_TM, D_A), tok),
            pl.BlockSpec((None, cpt, D_A, LANES), tr),
            pl.BlockSpec((None, cpt, D_B, LANES), tr),
            pl.BlockSpec((None, cpt, _GATE_ROWS, LANES), tr),
        ],
        out_shape=[
            nat(BF16), nat(BF16), nat(BF16), nat(BF16), nat(F32),
            jax.ShapeDtypeStruct((batch, nblk, D_A, LANES), BF16),
            jax.ShapeDtypeStruct((batch, nblk, D_B, LANES), BF16),
            jax.ShapeDtypeStruct((batch, nblk, _GATE_ROWS, LANES), F32),
        ],
        compiler_params=_params("parallel"),
        name="ab_proj",
    )(x, wnn, wnt, gate_b)


def _na_block_window_start(r0, rows):
    return np.clip(r0 - NA_WIN_H // 2, 0, rows - NA_KROWS)


def _na_block_classes(rows):
    reps = []
    cls_of_block = []
    seen = {}
    wh = min(NA_WIN_H, rows)
    for blk in range(rows // NA_ROWS_PER_BLOCK):
        r0 = blk * NA_ROWS_PER_BLOCK
        ws = _na_block_window_start(r0, rows)
        key = tuple(int(np.clip(r0 + d - wh // 2, 0, rows - wh)) - int(ws) for d in range(NA_ROWS_PER_BLOCK)) \
            + tuple(r0 + d - int(ws) for d in range(NA_ROWS_PER_BLOCK))
        if key not in seen:
            seen[key] = len(reps)
            reps.append(r0)
        cls_of_block.append(seen[key])
    return reps, cls_of_block


def _na_bias_tables(rpb, rows):
    reps, _ = _na_block_classes(rows)
    wh = min(NA_WIN_H, rows)
    n_dc = 2 * NA_WIN_W - 1
    col = np.arange(GRID_W)
    dc = np.clip(col[None, :] - col[:, None] + (NA_WIN_W - 1), 0, n_dc - 1)
    cs = np.clip(col - NA_WIN_W // 2, 0, GRID_W - NA_WIN_W)
    ok_c = (col[None, :] >= cs[:, None]) & (col[None, :] < cs[:, None] + NA_WIN_W)
    sel_dc = (dc[None] == np.arange(n_dc)[:, None, None]).astype(np.float32)
    by_col = jnp.einsum('...rd,dqk->...rqk', rpb, sel_dc, precision=lax.Precision.HIGHEST)
    by_col = jnp.where(ok_c, by_col, NA_MASKED).astype(F32)
    masked = jnp.full(by_col.shape[:-3] + (GRID_W, GRID_W), NA_MASKED, F32)
    tables = []
    for r0 in reps:
        ws = int(_na_block_window_start(r0, rows))
        q_rows = []
        for dq in range(NA_ROWS_PER_BLOCK):
            qr = r0 + dq
            rs = int(np.clip(qr - wh // 2, 0, rows - wh))
            slabs = [by_col[..., ws + i - qr + (NA_WIN_H - 1), :, :] if rs <= ws + i < rs + wh else masked
                     for i in range(NA_KROWS)]
            q_rows.append(jnp.concatenate(slabs, axis=-1))
        tables.append(jnp.concatenate(q_rows, axis=-2))
    return jnp.stack(tables, axis=-3)


def _na_kernel(q_ref, kt_ref, v_ref, bias_ref, o_ref, *, rows, cls_of_block):
    lane = lax.broadcasted_iota(jnp.int32, (NA_Q, LANES), 1)
    first_head = lane < NA_HEAD_DIM
    kblocks = NA_K // LANES

    def body(it, carry):
        blocks = []
        for u in range(NA_UNROLL):
            blk = it * NA_UNROLL + u
            r0 = blk * NA_ROWS_PER_BLOCK
            ws = jnp.clip(r0 - NA_WIN_H // 2, 0, rows - NA_KROWS)
            cls = jnp.int32(cls_of_block[-1])
            for b in range(len(cls_of_block) - 2, -1, -1):
                cls = jnp.where(blk <= b, jnp.int32(cls_of_block[b]), cls)
            kb0 = ws * GRID_W // LANES
            blocks.append((pl.multiple_of(blk * NA_Q, NA_Q), kb0, pl.multiple_of(kb0 * LANES, LANES), cls))
        scores = []
        for q0, kb0, _, cls in blocks:
            q2 = q_ref[pl.ds(q0, NA_Q), :]
            kwin = jnp.concatenate([kt_ref[kb0 + c] for c in range(kblocks)], axis=1)
            for h in range(2):
                qh = jnp.where(first_head if h == 0 else jnp.logical_not(first_head), q2, jnp.zeros_like(q2))
                scores.append(_dot(qh, kwin) + bias_ref[h, cls])
        probs = []
        for s in scores:
            p = jnp.exp(s - jnp.max(s, axis=-1, keepdims=True))
            probs.append((p.astype(BF16), jnp.sum(p, axis=-1, keepdims=True)))
        for u, (q0, _, k0, _) in enumerate(blocks):
            vwin = v_ref[pl.ds(k0, NA_K), :]
            outs = [_dot(p, vwin) / l for p, l in probs[2 * u:2 * u + 2]]
            o_ref[pl.ds(q0, NA_Q), :] = jnp.where(first_head, outs[0], outs[1]).astype(BF16)
        return carry

    lax.fori_loop(0, rows // (NA_ROWS_PER_BLOCK * NA_UNROLL), body, 0)


def _na_attention(qa, kat, va, bias, e, batch, seq):
    n = qa.shape[0]
    rows = seq // GRID_W
    _, cls_of_block = _na_block_classes(rows)
    ncls = bias.shape[2]
    pairs = NA_HEADS // 2
    return pl.pallas_call(
        functools.partial(_na_kernel, rows=rows, cls_of_block=tuple(cls_of_block)),
        grid=(pairs, batch),
        in_specs=[
            pl.BlockSpec((seq, LANES), lambda p, b: (b, p)),
            pl.BlockSpec((None, seq // LANES, LANES, LANES), lambda p, b: (b, 0, p, 0)),
            pl.BlockSpec((seq, LANES), lambda p, b: (b, p)),
            pl.BlockSpec((None, 2, ncls, NA_Q, NA_K), lambda p, b: (e * pairs + p, 0, 0, 0, 0)),
        ],
        out_specs=pl.BlockSpec((seq, LANES), lambda p, b: (b, p)),
        out_shape=jax.ShapeDtypeStruct((n, D_A), BF16),
        compiler_params=_params("parallel", "parallel"),
        name="na_attention",
    )(qa, kat, va, bias)


def _log_sigmoid(x):
    return jnp.minimum(x, 0.0) - jnp.log1p(jnp.exp(-jnp.abs(x)))


def _mlstm_kernel(q_ref, kt_ref, v_ref, ob_ref, gt_ref, gn_ref, o_ref,
                  h_ref, src_ref, srcmax_ref, gtot_ref, bcols_ref, *, seq):
    L = ML_CHUNK
    nc = seq // L
    assert nc % 2 == 0
    cb = L // LANES
    log_scale = -0.5 * float(np.log(ML_HEAD_DIM))
    row_i = lax.broadcasted_iota(jnp.int32, (L, L), 0)
    col_i = lax.broadcasted_iota(jnp.int32, (L, L), 1)
    ones_col = (lax.broadcasted_iota(jnp.int32, (L, LANES), 1) == 0).astype(BF16)

    def tri_mask(reverse):
        return (col_i >= row_i) if reverse else (col_i <= row_i)

    def stat_row(c, head, reverse):
        return (head * nc + c) * SUBLANES + (2 if reverse else 0)

    def gate_prologue(head):
        hrows = slice(head * SUBLANES, (head + 1) * SUBLANES)
        x = jnp.concatenate(
            [jnp.concatenate([gt_ref[c * cb + j, hrows, :] for j in range(cb)], axis=1) for c in range(nc)],
            axis=0)
        n = nc * SUBLANES
        kind = lax.broadcasted_iota(jnp.int32, (n, 1), 0) % SUBLANES
        logf = _log_sigmoid(x)

        def split3(a):
            hi = a.astype(BF16)
            rest = a - hi.astype(F32)
            mid = rest.astype(BF16)
            return hi, mid, (rest - mid.astype(F32)).astype(BF16)

        pieces = split3(logf)
        pad = jnp.zeros((LANES - n, L), F32)
        pieces_t = split3(jnp.concatenate([logf, pad], axis=0).T)
        lower = tri_mask(False).astype(BF16)
        upper = tri_mask(True).astype(BF16)
        b_rows_f = sum(_dot(p, upper) for p in pieces)
        b_rows_r = sum(_dot(p, lower) for p in pieces)
        b_cols_f = sum(_dot(lower, p) for p in pieces_t)
        b_cols_r = sum(_dot(upper, p) for p in pieces_t)
        up1 = lambda a: pltpu.roll(a, n - 1, axis=0)
        src = x - jnp.where(kind == 0, up1(b_rows_f), up1(b_rows_r))
        rows = pl.ds(head * n, n)
        src_ref[rows, :] = src
        srcmax_ref[rows, :] = jnp.broadcast_to(jnp.max(src, axis=1, keepdims=True), (n, L))
        gtot_ref[rows, :] = up1(jnp.broadcast_to(jnp.sum(logf, axis=1, keepdims=True), (n, L)))
        bcols_ref[2 * head] = b_cols_f
        bcols_ref[2 * head + 1] = b_cols_r

    def stage_load(c, carry, head, reverse):
        rows = slice(c * L, (c + 1) * L)
        lanes = slice(head * ML_HEAD_DIM, (head + 1) * ML_HEAD_DIM)
        b_lane = c * SUBLANES + (3 if reverse else 1)
        st = dict(
            tri=tri_mask(reverse), row=stat_row(c, head, reverse),
            b_col=bcols_ref[2 * head + int(reverse), :, b_lane:b_lane + 1],
            q=q_ref[rows, lanes],
            kt=jnp.concatenate([kt_ref[c * cb + j, lanes, :] for j in range(cb)], axis=1),
            v_ext=jnp.concatenate([v_ref[rows, lanes], ones_col], axis=1),
            c_ext=carry[0], m=carry[1])
        st['qk'] = _dot(st['q'], st['kt'])
        st['qc'] = _dot(st['q'], st['c_ext'].astype(BF16))
        return st

    def stage_gates(st):
        m, slot = st['m'], pl.ds(st['row'], 1)
        src = src_ref[slot, :]
        mm = jnp.maximum(m, srcmax_ref[slot, :])
        st['w_src'] = jnp.exp(src + (log_scale - mm))
        st['w_carry'] = jnp.exp(m - mm)[:, :1]
        st['floor'] = jnp.exp(-(st['b_col'] + mm[:, :1]))
        st['m_new'] = gtot_ref[slot, :] + mm

    def stage_state(st):
        kv = _dot((st['kt'].astype(F32) * st['w_src']).astype(BF16), st['v_ext'])
        return st['w_carry'] * st['c_ext'] + kv, st['m_new']

    def stage_out(st):
        qkw = jnp.where(st['tri'], st['qk'] * st['w_src'], 0.0)
        tot = _dot(qkw.astype(BF16), st['v_ext']) + st['w_carry'] * st['qc']
        num = tot[:, :ML_HEAD_DIM]
        den = tot[:, ML_HEAD_DIM:ML_HEAD_DIM + 1]
        return num / jnp.maximum(jnp.abs(den), st['floor'])

    def finish(c, head, h):
        rows = slice(c * L, (c + 1) * L)
        lanes = slice(head * ML_HEAD_DIM, (head + 1) * ML_HEAD_DIM)
        mu = jnp.mean(h, axis=-1, keepdims=True)
        hc = h - mu
        var = jnp.mean(hc * hc, axis=-1, keepdims=True)
        hn = hc * lax.rsqrt(var + LN_EPS)
        gate = jax.nn.sigmoid(ob_ref[rows, lanes])
        o_ref[rows, lanes] = (gate * hn * gn_ref[head]).astype(BF16)

    def body(ci, carry, second_half):
        streams = []
        for head in range(ML_HEADS_PER_STEP):
            streams.append((ci, head, stage_load(ci, carry[2 * head], head, False)))
            streams.append((nc - 1 - ci, head, stage_load(nc - 1 - ci, carry[2 * head + 1], head, True)))
        for _, _, st in streams:
            stage_gates(st)
        new_carry = tuple(stage_state(st) for _, _, st in streams)
        for c, head, st in streams:
            h = stage_out(st)
            rows = slice(c * L, (c + 1) * L)
            lanes = slice(head * ML_HEAD_DIM, (head + 1) * ML_HEAD_DIM)
            if second_half:
                finish(c, head, h + h_ref[rows, lanes])
            else:
                h_ref[rows, lanes] = h
        return new_carry

    for head in range(ML_HEADS_PER_STEP):
        gate_prologue(head)

    zero = (jnp.zeros((ML_HEAD_DIM, 2 * LANES), F32), jnp.zeros((1, L), F32))
    carry = (zero,) * (2 * ML_HEADS_PER_STEP)
    for ci in range(nc):
        carry = body(ci, carry, second_half=ci >= nc // 2)


def _mlstm(qb, kbt, vb, ob, gt, gn_g, e, batch, seq):
    n = qb.shape[0]
    hps = ML_HEADS_PER_STEP
    width = hps * ML_HEAD_DIM
    tok = lambda b, g: (b, g)
    return pl.pallas_call(
        functools.partial(_mlstm_kernel, seq=seq),
        grid=(batch, ML_HEADS // hps),
        in_specs=[
            pl.BlockSpec((seq, width), tok),
            pl.BlockSpec((None, seq // LANES, width, LANES), lambda b, g: (b, 0, g, 0)),
            pl.BlockSpec((seq, width), tok),
            pl.BlockSpec((seq, width), tok),
            pl.BlockSpec((None, seq // LANES, hps * SUBLANES, LANES), lambda b, g: (b, 0, g, 0)),
            pl.BlockSpec((hps, 1, ML_HEAD_DIM), lambda b, g: (e * (ML_HEADS // hps) + g, 0, 0)),
        ],
        out_specs=pl.BlockSpec((seq, width), tok),
        out_shape=jax.ShapeDtypeStruct((n, D_B), BF16),
        scratch_shapes=[pltpu.VMEM((seq, width), F32)]
        + [pltpu.VMEM((hps * (seq // ML_CHUNK) * SUBLANES, ML_CHUNK), F32)] * 3
        + [pltpu.VMEM((2 * hps, ML_CHUNK, LANES), F32)],
        compiler_params=_params("parallel", "parallel"),
        name="mlstm",
    )(qb, kbt, vb, ob, gt, gn_g)


def _ab_out_kernel(x_ref, ya_ref, yb_ref, w_ref, g_ref, b_ref, o_ref):
    mix = _dot(ya_ref[...], w_ref[:D_A, :]) + _dot(yb_ref[...], w_ref[D_A:, :])
    o_ref[...] = _layer_norm(DEEPNORM_ALPHA * x_ref[...] + mix, g_ref[...], b_ref[...])


def _ab_out_ln(x, ya, yb, w_out, ln_g, ln_b, e, ln_idx):
    n = x.shape[0]
    tok = lambda i: (i, 0)
    return pl.pallas_call(
        _ab_out_kernel,
        grid=(n // PROJ_TM,),
        in_specs=[
            pl.BlockSpec((PROJ_TM, D_MODEL), tok),
            pl.BlockSpec((PROJ_TM, D_A), tok),
            pl.BlockSpec((PROJ_TM, D_B), tok),
            pl.BlockSpec((None, D_A + D_B, D_MODEL), lambda i: (e, 0, 0)),
            pl.BlockSpec((None, 1, D_MODEL), lambda i: (ln_idx, 0, 0)),
            pl.BlockSpec((None, 1, D_MODEL), lambda i: (ln_idx, 0, 0)),
        ],
        out_specs=pl.BlockSpec((PROJ_TM, D_MODEL), tok),
        out_shape=jax.ShapeDtypeStruct((n, D_MODEL), F32),
        compiler_params=_params("parallel"),
        name="ab_out_ln",
    )(x, ya, yb, w_out, ln_g, ln_b)


def _conv_kernel(x_ref, xp_ref, xn_ref, win_ref, cw_ref, cb_ref, wout_ref, g_ref, b_ref, o_ref,
                 *, tiles_per_seq):
    i = pl.program_id(0)
    D = D_MODEL
    x = x_ref[...]
    xb = x.astype(BF16)
    xe = jnp.concatenate([xp_ref[...], x, xn_ref[...]], axis=0).astype(BF16)
    ue = _dot(xe, win_ref[:, D:2 * D]) * _dot(xe, win_ref[:, 2 * D:])
    ext = PROJ_TM + 2 * SUBLANES
    tile = slice(SUBLANES, SUBLANES + PROJ_TM)
    u = ue[tile]
    u_m1 = pltpu.roll(ue, 1, axis=0)[tile]
    u_p1 = pltpu.roll(ue, ext - 1, axis=0)[tile]
    pos = i % tiles_per_seq
    row = lax.broadcasted_iota(jnp.int32, (PROJ_TM, 1), 0)
    u_m1 = jnp.where(jnp.logical_and(row == 0, pos == 0), 0.0, u_m1)
    u_p1 = jnp.where(jnp.logical_and(row == PROJ_TM - 1, pos == tiles_per_seq - 1), 0.0, u_p1)
    y = cw_ref[0:1, :] * u_m1 + cw_ref[1:2, :] * u + cw_ref[2:3, :] * u_p1 + cb_ref[...]
    z = (_dot(xb, win_ref[:, :D]) * y).astype(BF16)
    mix = _dot(z, wout_ref[...])
    o_ref[...] = _layer_norm(DEEPNORM_ALPHA * x + mix, g_ref[...], b_ref[...])


def _conv_mixer_ln(x, w_in, conv_w, conv_b, w_out, ln_g, ln_b, o, ln_idx, seq):
    n = x.shape[0]
    tiles_per_seq = seq // PROJ_TM
    rb = PROJ_TM // SUBLANES
    last_rb = n // SUBLANES - 1
    tok = lambda i: (i, 0)
    return pl.pallas_call(
        functools.partial(_conv_kernel, tiles_per_seq=tiles_per_seq),
        grid=(n // PROJ_TM,),
        in_specs=[
            pl.BlockSpec((PROJ_TM, D_MODEL), tok),
            pl.BlockSpec((SUBLANES, D_MODEL), lambda i: (jnp.maximum(i * rb - 1, 0), 0)),
            pl.BlockSpec((SUBLANES, D_MODEL), lambda i: (jnp.minimum((i + 1) * rb, last_rb), 0)),
            pl.BlockSpec((None, D_MODEL, 3 * D_MODEL), lambda i: (o, 0, 0)),
            pl.BlockSpec((None, 3, D_MODEL), lambda i: (o, 0, 0)),
            pl.BlockSpec((None, 1, D_MODEL), lambda i: (o, 0, 0)),
            pl.BlockSpec((None, D_MODEL, D_MODEL), lambda i: (o, 0, 0)),
            pl.BlockSpec((None, 1, D_MODEL), lambda i: (ln_idx, 0, 0)),
            pl.BlockSpec((None, 1, D_MODEL), lambda i: (ln_idx, 0, 0)),
        ],
        out_specs=pl.BlockSpec((PROJ_TM, D_MODEL), tok),
        out_shape=jax.ShapeDtypeStruct((n, D_MODEL), F32),
        compiler_params=_params("parallel"),
        name="conv_mixer_ln",
    )(x, x, x, w_in, conv_w, conv_b, w_out, ln_g, ln_b)


def _prep_ab_weights(ab_w_in, ab_gate_b):
    n_even = ab_w_in.shape[0]
    w = ab_w_in
    cut = lambda lo, hi: w[:, :, lo:hi]
    qa, ka, va = cut(0, D_A), cut(D_A, 2 * D_A), cut(2 * D_A, 3 * D_A)
    o0 = 3 * D_A
    qb, kb, vb, ob = (cut(o0, o0 + D_B), cut(o0 + D_B, o0 + 2 * D_B),
                      cut(o0 + 2 * D_B, o0 + 3 * D_B), cut(o0 + 3 * D_B, o0 + 4 * D_B))
    gates = cut(o0 + 4 * D_B, o0 + 4 * D_B + 4 * ML_HEADS)
    gates = gates.reshape(n_even, D_MODEL, 4, ML_HEADS).transpose(0, 3, 2, 1)
    gates = jnp.pad(gates, ((0, 0), (0, 0), (0, SUBLANES - 4), (0, 0))).reshape(n_even, _GATE_ROWS, D_MODEL)
    wnn = jnp.concatenate([qa, va, qb, vb, ob], axis=2).astype(BF16)
    wnt = jnp.concatenate([ka.transpose(0, 2, 1), kb.transpose(0, 2, 1), gates], axis=1).astype(BF16)
    gb = jnp.pad(ab_gate_b.astype(F32).transpose(0, 2, 1), ((0, 0), (0, 0), (0, SUBLANES - 4)))
    return wnn, wnt, gb.reshape(n_even, _GATE_ROWS, 1)


def kernel(x, ln_g, ln_b, ffn_w_in, ffn_w_out, ab_w_in, ab_gate_b, na_rpb, ml_gn_g, ab_w_out,
           sc_w_in, sc_conv_w, sc_conv_b, sc_w_out):
    batch, seq, d = x.shape
    rows = seq // GRID_W
    h = x.reshape(batch * seq, d)
    ffn_in = ffn_w_in.astype(BF16)
    ffn_out = ffn_w_out.astype(BF16)
    lng = ln_g.astype(F32).reshape(DEPTH * 3, 1, d)
    lnb = ln_b.astype(F32).reshape(DEPTH * 3, 1, d)
    wnn, wnt, gate_b = _prep_ab_weights(ab_w_in, ab_gate_b)
    ab_out = ab_w_out.astype(BF16)
    gn = ml_gn_g.astype(F32).reshape(-1, 1, ML_HEAD_DIM)
    sc_in = sc_w_in.astype(BF16)
    sc_out = sc_w_out.astype(BF16)
    conv_b = sc_conv_b.astype(F32).reshape(-1, 1, d)
    conv_w = sc_conv_w.astype(F32)
    na_bias = _na_bias_tables(na_rpb.astype(F32), rows)
    na_bias = na_bias.reshape(-1, 2, *na_bias.shape[2:])

    for layer in range(DEPTH):
        h = _ffn_ln(h, ffn_in, ffn_out, lng, lnb, layer, 0, layer * 3)
        if layer % 2 == 0:
            e = layer // 2
            qa, va, qb, vb, ob, kat, kbt, gt = _ab_proj(h, wnn, wnt, gate_b, e, batch, seq)
            ya = _na_attention(qa, kat, va, na_bias, e, batch, seq)
            yb = _mlstm(qb, kbt, vb, ob, gt, gn, e, batch, seq)
            h = _ab_out_ln(h, ya, yb, ab_out, lng, lnb, e, layer * 3 + 1)
        else:
            o = layer // 2
            h = _conv_mixer_ln(h, sc_in, conv_w, conv_b, sc_out, lng, lnb, o, layer * 3 + 1, seq)
        h = _ffn_ln(h, ffn_in, ffn_out, lng, lnb, layer, 1, layer * 3 + 2)
    return h.reshape(batch, seq, d)
```

```python
import functools

import numpy as np
import jax
import jax.numpy as jnp
from jax import lax
from jax.experimental import pallas as pl
from jax.experimental.pallas import tpu as pltpu

D_MODEL = 1024
DEPTH = 4
GRID_W = 64
NA_HEAD_DIM = 64
NA_HEADS = 8
NA_WIN_H = 8
NA_WIN_W = 16
ML_HEADS = 4
ML_HEAD_DIM = 128
D_A = NA_HEADS * NA_HEAD_DIM
D_B = ML_HEADS * ML_HEAD_DIM
D_FF = 2816
DEEPNORM_ALPHA = (2 * DEPTH) ** 0.25
LN_EPS = 1e-5

LANES = 128
SUBLANES = 8
VMEM_LIMIT = 52 * 1024 * 1024

FFN_TM = 512
FFN_TF = 256
PROJ_TM = 512
ML_CHUNK = 256
ML_HEADS_PER_STEP = 2
NA_ROWS_PER_BLOCK = 2
NA_Q = NA_ROWS_PER_BLOCK * GRID_W
NA_KROWS = 10
NA_K = NA_KROWS * GRID_W
NA_MASKED = -1e30
NA_UNROLL = 1

BF16 = jnp.bfloat16
F32 = jnp.float32


def _dot(a, b):
    return jnp.dot(a, b, preferred_element_type=F32)


def _layer_norm(y, g, b):
    mu = jnp.mean(y, axis=-1, keepdims=True)
    yc = y - mu
    var = jnp.mean(yc * yc, axis=-1, keepdims=True)
    return yc * lax.rsqrt(var + LN_EPS) * g + b


def _params(*sem):
    return pltpu.CompilerParams(dimension_semantics=sem, vmem_limit_bytes=VMEM_LIMIT)


def _ffn_kernel(x_ref, wg_ref, wu_ref, wo_ref, g_ref, b_ref, o_ref):
    x = x_ref[...]
    xb = x.astype(BF16)
    acc = None
    for c in range(D_FF // FFN_TF):
        cols = slice(c * FFN_TF, (c + 1) * FFN_TF)
        gate = _dot(xb, wg_ref[:, cols])
        up = _dot(xb, wu_ref[:, cols])
        act = (gate * jax.nn.sigmoid(gate) * up).astype(BF16)
        part = _dot(act, wo_ref[cols, :])
        acc = part if acc is None else acc + part
    o_ref[...] = _layer_norm(DEEPNORM_ALPHA * x + 0.5 * acc, g_ref[...], b_ref[...])


def _resident(block_shape, index_map):
    return pl.BlockSpec(block_shape, index_map, pipeline_mode=pl.Buffered(1))


def _ffn_ln(x, w_in, w_out, ln_g, ln_b, layer, k, ln_idx):
    n = x.shape[0]
    return pl.pallas_call(
        _ffn_kernel,
        grid=(n // FFN_TM,),
        in_specs=[
            pl.BlockSpec((FFN_TM, D_MODEL), lambda i: (i, 0)),
            _resident((None, None, D_MODEL, D_FF), lambda i: (layer, k, 0, 0)),
            _resident((None, None, D_MODEL, D_FF), lambda i: (layer, k, 0, 1)),
            _resident((None, None, D_FF, D_MODEL), lambda i: (layer, k, 0, 0)),
            pl.BlockSpec((None, 1, D_MODEL), lambda i: (ln_idx, 0, 0)),
            pl.BlockSpec((None, 1, D_MODEL), lambda i: (ln_idx, 0, 0)),
        ],
        out_specs=pl.BlockSpec((FFN_TM, D_MODEL), lambda i: (i, 0)),
        out_shape=jax.ShapeDtypeStruct((n, D_MODEL), F32),
        compiler_params=_params("parallel"),
        name="ffn_ln",
    )(x, w_in, w_in, w_out, ln_g, ln_b)


_NN_GROUPS = 5
_GATE_ROWS = ML_HEADS * SUBLANES


def _ab_proj_kernel(x_ref, wnn_ref, wnt_ref, gb_ref,
                    qa_ref, va_ref, qb_ref, vb_ref, ob_ref, kat_ref, kbt_ref, gt_ref):
    xb = x_ref[...].astype(BF16)

    def nn(g):
        return _dot(xb, wnn_ref[:, g * D_A:(g + 1) * D_A])

    def nt(lo, hi):
        return lax.dot_general(wnt_ref[lo:hi, :], xb, (((1,), (1,)), ((), ())),
                               preferred_element_type=F32)

    qa_ref[...] = (nn(0) * (NA_HEAD_DIM ** -0.5)).astype(BF16)
    va_ref[...] = nn(1).astype(BF16)
    qb_ref[...] = nn(2).astype(BF16)
    vb_ref[...] = nn(3).astype(BF16)
    ob_ref[...] = nn(4)
    kat = nt(0, D_A).astype(BF16)
    kbt = nt(D_A, D_A + D_B).astype(BF16)
    gt = nt(D_A + D_B, D_A + D_B + _GATE_ROWS) + gb_ref[...]
    for c in range(PROJ_TM // LANES):
        kat_ref[c] = kat[:, c * LANES:(c + 1) * LANES]
        kbt_ref[c] = kbt[:, c * LANES:(c + 1) * LANES]
        gt_ref[c] = gt[:, c * LANES:(c + 1) * LANES]


def _ab_proj(x, wnn, wnt, gate_b, e, batch, seq):
    n = x.shape[0]
    tiles_per_seq = seq // PROJ_TM
    cpt = PROJ_TM // LANES
    nblk = seq // LANES
    nt_rows = D_A + D_B + _GATE_ROWS
    tok = lambda i: (i, 0)
    tr = lambda i: (i // tiles_per_seq, i % tiles_per_seq, 0, 0)
    nat = lambda dt: jax.ShapeDtypeStruct((n, D_A), dt)
    return pl.pallas_call(
        _ab_proj_kernel,
        grid=(n // PROJ_TM,),
        in_specs=[
            pl.BlockSpec((PROJ_TM, D_MODEL), tok),
            pl.BlockSpec((None, D_MODEL, _NN_GROUPS * D_A), lambda i: (e, 0, 0)),
            pl.BlockSpec((None, nt_rows, D_MODEL), lambda i: (e, 0, 0)),
            pl.BlockSpec((None, _GATE_ROWS, 1), lambda i: (e, 0, 0)),
        ],
        out_specs=[
            pl.BlockSpec((PROJ_TM, D_A), tok),
            pl.BlockSpec((PROJ_TM, D_A), tok),
            pl.BlockSpec((PROJ_TM, D_A), tok),
            pl.BlockSpec((PROJ_TM, D_A), tok),
            pl.BlockSpec((PROJ_TM, D_A), tok),
            pl.BlockSpec((None, cpt, D_A, LANES), tr),
            pl.BlockSpec((None, cpt, D_B, LANES), tr),
            pl.BlockSpec((None, cpt, _GATE_ROWS, LANES), tr),
        ],
        out_shape=[
            nat(BF16), nat(BF16), nat(BF16), nat(BF16), nat(F32),
            jax.ShapeDtypeStruct((batch, nblk, D_A, LANES), BF16),
            jax.ShapeDtypeStruct((batch, nblk, D_B, LANES), BF16),
            jax.ShapeDtypeStruct((batch, nblk, _GATE_ROWS, LANES), F32),
        ],
        compiler_params=_params("parallel"),
        name="ab_proj",
    )(x, wnn, wnt, gate_b)


def _na_block_window_start(r0, rows):
    return np.clip(r0 - NA_WIN_H // 2, 0, rows - NA_KROWS)


def _na_block_classes(rows):
    reps = []
    cls_of_block = []
    seen = {}
    wh = min(NA_WIN_H, rows)
    for blk in range(rows // NA_ROWS_PER_BLOCK):
        r0 = blk * NA_ROWS_PER_BLOCK
        ws = _na_block_window_start(r0, rows)
        key = tuple(int(np.clip(r0 + d - wh // 2, 0, rows - wh)) - int(ws) for d in range(NA_ROWS_PER_BLOCK)) \
            + tuple(r0 + d - int(ws) for d in range(NA_ROWS_PER_BLOCK))
        if key not in seen:
            seen[key] = len(reps)
            reps.append(r0)
        cls_of_block.append(seen[key])
    return reps, cls_of_block


def _na_bias_tables(rpb, rows):
    reps, _ = _na_block_classes(rows)
    wh = min(NA_WIN_H, rows)
    n_dc = 2 * NA_WIN_W - 1
    col = np.arange(GRID_W)
    dc = np.clip(col[None, :] - col[:, None] + (NA_WIN_W - 1), 0, n_dc - 1)
    cs = np.clip(col - NA_WIN_W // 2, 0, GRID_W - NA_WIN_W)
    ok_c = (col[None, :] >= cs[:, None]) & (col[None, :] < cs[:, None] + NA_WIN_W)
    sel_dc = (dc[None] == np.arange(n_dc)[:, None, None]).astype(np.float32)
    by_col = jnp.einsum('...rd,dqk->...rqk', rpb, sel_dc, precision=lax.Precision.HIGHEST)
    by_col = jnp.where(ok_c, by_col, NA_MASKED).astype(F32)
    masked = jnp.full(by_col.shape[:-3] + (GRID_W, GRID_W), NA_MASKED, F32)
    tables = []
    for r0 in reps:
        ws = int(_na_block_window_start(r0, rows))
        q_rows = []
        for dq in range(NA_ROWS_PER_BLOCK):
            qr = r0 + dq
            rs = int(np.clip(qr - wh // 2, 0, rows - wh))
            slabs = [by_col[..., ws + i - qr + (NA_WIN_H - 1), :, :] if rs <= ws + i < rs + wh else masked
                     for i in range(NA_KROWS)]
            q_rows.append(jnp.concatenate(slabs, axis=-1))
        tables.append(jnp.concatenate(q_rows, axis=-2))
    return jnp.stack(tables, axis=-3)


def _na_kernel(q_ref, kt_ref, v_ref, bias_ref, o_ref, *, rows, cls_of_block):
    lane = lax.broadcasted_iota(jnp.int32, (NA_Q, LANES), 1)
    first_head = lane < NA_HEAD_DIM
    kblocks = NA_K // LANES

    def window_start(blk):
        return int(_na_block_window_start(blk * NA_ROWS_PER_BLOCK, rows))

    def score_stage(group):
        scores = []
        for blk in group:
            q2 = q_ref[blk * NA_Q:(blk + 1) * NA_Q, :]
            kb0 = window_start(blk) * GRID_W // LANES
            kwin = jnp.concatenate([kt_ref[kb0 + c] for c in range(kblocks)], axis=1)
            for h in range(2):
                qh = jnp.where(first_head if h == 0 else jnp.logical_not(first_head), q2, jnp.zeros_like(q2))
                scores.append(_dot(qh, kwin) + bias_ref[h, cls_of_block[blk]])
        return scores

    def output_stage(group, scores):
        probs = []
        for s in scores:
            p = jnp.exp(s - jnp.max(s, axis=-1, keepdims=True))
            probs.append((p.astype(BF16), jnp.sum(p, axis=-1, keepdims=True)))
        for u, blk in enumerate(group):
            k0 = window_start(blk) * GRID_W
            vwin = v_ref[k0:k0 + NA_K, :]
            outs = [_dot(p, vwin) / l for p, l in probs[2 * u:2 * u + 2]]
            o_ref[blk * NA_Q:(blk + 1) * NA_Q, :] = jnp.where(first_head, outs[0], outs[1]).astype(BF16)

    nblk = rows // NA_ROWS_PER_BLOCK
    groups = [range(g, g + NA_UNROLL) for g in range(0, nblk, NA_UNROLL)]
    scores = score_stage(groups[0])
    for g, group in enumerate(groups):
        ahead = score_stage(groups[g + 1]) if g + 1 < len(groups) else None
        output_stage(group, scores)
        scores = ahead


def _na_attention(qa, kat, va, bias, e, batch, seq):
    n = qa.shape[0]
    rows = seq // GRID_W
    _, cls_of_block = _na_block_classes(rows)
    ncls = bias.shape[2]
    pairs = NA_HEADS // 2
    return pl.pallas_call(
        functools.partial(_na_kernel, rows=rows, cls_of_block=tuple(cls_of_block)),
        grid=(pairs, batch),
        in_specs=[
            pl.BlockSpec((seq, LANES), lambda p, b: (b, p)),
            pl.BlockSpec((None, seq // LANES, LANES, LANES), lambda p, b: (b, 0, p, 0)),
            pl.BlockSpec((seq, LANES), lambda p, b: (b, p)),
            pl.BlockSpec((None, 2, ncls, NA_Q, NA_K), lambda p, b: (e * pairs + p, 0, 0, 0, 0)),
        ],
        out_specs=pl.BlockSpec((seq, LANES), lambda p, b: (b, p)),
        out_shape=jax.ShapeDtypeStruct((n, D_A), BF16),
        compiler_params=_params("parallel", "parallel"),
        name="na_attention",
    )(qa, kat, va, bias)


def _log_sigmoid(x):
    return jnp.minimum(x, 0.0) - jnp.log1p(jnp.exp(-jnp.abs(x)))


def _mlstm_kernel(q_ref, kt_ref, v_ref, ob_ref, gt_ref, gn_ref, o_ref,
                  h_ref, src_ref, srcmax_ref, gtot_ref, bcols_ref, *, seq):
    L = ML_CHUNK
    nc = seq // L
    assert nc % 2 == 0
    cb = L // LANES
    log_scale = -0.5 * float(np.log(ML_HEAD_DIM))
    row_i = lax.broadcasted_iota(jnp.int32, (L, L), 0)
    col_i = lax.broadcasted_iota(jnp.int32, (L, L), 1)
    ones_col = (lax.broadcasted_iota(jnp.int32, (L, LANES), 1) == 0).astype(BF16)

    def tri_mask(reverse):
        return (col_i >= row_i) if reverse else (col_i <= row_i)

    def stat_row(c, head, reverse):
        return (head * nc + c) * SUBLANES + (2 if reverse else 0)

    def gate_prologue(head):
        hrows = slice(head * SUBLANES, (head + 1) * SUBLANES)
        x = jnp.concatenate(
            [jnp.concatenate([gt_ref[c * cb + j, hrows, :] for j in range(cb)], axis=1) for c in range(nc)],
            axis=0)
        n = nc * SUBLANES
        kind = lax.broadcasted_iota(jnp.int32, (n, 1), 0) % SUBLANES
        logf = _log_sigmoid(x)

        def split3(a):
            hi = a.astype(BF16)
            rest = a - hi.astype(F32)
            mid = rest.astype(BF16)
            return hi, mid, (rest - mid.astype(F32)).astype(BF16)

        pieces = split3(logf)
        pad = jnp.zeros((LANES - n, L), F32)
        pieces_t = split3(jnp.concatenate([logf, pad], axis=0).T)
        lower = tri_mask(False).astype(BF16)
        upper = tri_mask(True).astype(BF16)
        b_rows_f = sum(_dot(p, upper) for p in pieces)
        b_rows_r = sum(_dot(p, lower) for p in pieces)
        b_cols_f = sum(_dot(lower, p) for p in pieces_t)
        b_cols_r = sum(_dot(upper, p) for p in pieces_t)
        up1 = lambda a: pltpu.roll(a, n - 1, axis=0)
        src = x - jnp.where(kind == 0, up1(b_rows_f), up1(b_rows_r))
        rows = pl.ds(head * n, n)
        src_ref[rows, :] = src
        srcmax_ref[rows, :] = jnp.broadcast_to(jnp.max(src, axis=1, keepdims=True), (n, L))
        gtot_ref[rows, :] = up1(jnp.broadcast_to(jnp.sum(logf, axis=1, keepdims=True), (n, L)))
        bcols_ref[2 * head] = b_cols_f
        bcols_ref[2 * head + 1] = b_cols_r

    def stage_load(c, carry, head, reverse):
        rows = slice(c * L, (c + 1) * L)
        lanes = slice(head * ML_HEAD_DIM, (head + 1) * ML_HEAD_DIM)
        b_lane = c * SUBLANES + (3 if reverse else 1)
        st = dict(
            tri=tri_mask(reverse), row=stat_row(c, head, reverse),
            b_col=bcols_ref[2 * head + int(reverse), :, b_lane:b_lane + 1],
            q=q_ref[rows, lanes],
            kt=jnp.concatenate([kt_ref[c * cb + j, lanes, :] for j in range(cb)], axis=1),
            v_ext=jnp.concatenate([v_ref[rows, lanes], ones_col], axis=1),
            c_ext=carry[0], m=carry[1])
        st['qk'] = _dot(st['q'], st['kt'])
        st['qc'] = _dot(st['q'], st['c_ext'].astype(BF16))
        return st

    def stage_gates(st):
        m, slot = st['m'], pl.ds(st['row'], 1)
        src = src_ref[slot, :]
        mm = jnp.maximum(m, srcmax_ref[slot, :])
        st['w_src'] = jnp.exp(src + (log_scale - mm))
        st['w_carry'] = jnp.exp(m - mm)[:, :1]
        st['floor'] = jnp.exp(-(st['b_col'] + mm[:, :1]))
        st['m_new'] = gtot_ref[slot, :] + mm

    def stage_state(st):
        kv = _dot((st['kt'].astype(F32) * st['w_src']).astype(BF16), st['v_ext'])
        return st['w_carry'] * st['c_ext'] + kv, st['m_new']

    def stage_out(st):
        qkw = jnp.where(st['tri'], st['qk'] * st['w_src'], 0.0)
        tot = _dot(qkw.astype(BF16), st['v_ext']) + st['w_carry'] * st['qc']
        num = tot[:, :ML_HEAD_DIM]
        den = tot[:, ML_HEAD_DIM:ML_HEAD_DIM + 1]
        return num / jnp.maximum(jnp.abs(den), st['floor'])

    def finish(c, head, h):
        rows = slice(c * L, (c + 1) * L)
        lanes = slice(head * ML_HEAD_DIM, (head + 1) * ML_HEAD_DIM)
        mu = jnp.mean(h, axis=-1, keepdims=True)
        hc = h - mu
        var = jnp.mean(hc * hc, axis=-1, keepdims=True)
        hn = hc * lax.rsqrt(var + LN_EPS)
        gate = jax.nn.sigmoid(ob_ref[rows, lanes])
        o_ref[rows, lanes] = (gate * hn * gn_ref[head]).astype(BF16)

    def body(ci, carry, second_half):
        streams = []
        for head in range(ML_HEADS_PER_STEP):
            streams.append((ci, head, stage_load(ci, carry[2 * head], head, False)))
            streams.append((nc - 1 - ci, head, stage_load(nc - 1 - ci, carry[2 * head + 1], head, True)))
        for _, _, st in streams:
            stage_gates(st)
        new_carry = tuple(stage_state(st) for _, _, st in streams)
        for c, head, st in streams:
            h = stage_out(st)
            rows = slice(c * L, (c + 1) * L)
            lanes = slice(head * ML_HEAD_DIM, (head + 1) * ML_HEAD_DIM)
            if second_half:
                finish(c, head, h + h_ref[rows, lanes])
            else:
                h_ref[rows, lanes] = h
        return new_carry

    for head in range(ML_HEADS_PER_STEP):
        gate_prologue(head)

    zero = (jnp.zeros((ML_HEAD_DIM, 2 * LANES), F32), jnp.zeros((1, L), F32))
    carry = (zero,) * (2 * ML_HEADS_PER_STEP)
    for ci in range(nc):
        carry = body(ci, carry, second_half=ci >= nc // 2)


def _mlstm(qb, kbt, vb, ob, gt, gn_g, e, batch, seq):
    n = qb.shape[0]
    hps = ML_HEADS_PER_STEP
    width = hps * ML_HEAD_DIM
    tok = lambda b, g: (b, g)
    return pl.pallas_call(
        functools.partial(_mlstm_kernel, seq=seq),
        grid=(batch, ML_HEADS // hps),
        in_specs=[
            pl.BlockSpec((seq, width), tok),
            pl.BlockSpec((None, seq // LANES, width, LANES), lambda b, g: (b, 0, g, 0)),
            pl.BlockSpec((seq, width), tok),
            pl.BlockSpec((seq, width), tok),
            pl.BlockSpec((None, seq // LANES, hps * SUBLANES, LANES), lambda b, g: (b, 0, g, 0)),
            pl.BlockSpec((hps, 1, ML_HEAD_DIM), lambda b, g: (e * (ML_HEADS // hps) + g, 0, 0)),
        ],
        out_specs=pl.BlockSpec((seq, width), tok),
        out_shape=jax.ShapeDtypeStruct((n, D_B), BF16),
        scratch_shapes=[pltpu.VMEM((seq, width), F32)]
        + [pltpu.VMEM((hps * (seq // ML_CHUNK) * SUBLANES, ML_CHUNK), F32)] * 3
        + [pltpu.VMEM((2 * hps, ML_CHUNK, LANES), F32)],
        compiler_params=_params("parallel", "parallel"),
        name="mlstm",
    )(qb, kbt, vb, ob, gt, gn_g)


def _ab_out_kernel(x_ref, ya_ref, yb_ref, w_ref, g_ref, b_ref, o_ref):
    mix = _dot(ya_ref[...], w_ref[:D_A, :]) + _dot(yb_ref[...], w_ref[D_A:, :])
    o_ref[...] = _layer_norm(DEEPNORM_ALPHA * x_ref[...] + mix, g_ref[...], b_ref[...])


def _ab_out_ln(x, ya, yb, w_out, ln_g, ln_b, e, ln_idx):
    n = x.shape[0]
    tok = lambda i: (i, 0)
    return pl.pallas_call(
        _ab_out_kernel,
        grid=(n // PROJ_TM,),
        in_specs=[
            pl.BlockSpec((PROJ_TM, D_MODEL), tok),
            pl.BlockSpec((PROJ_TM, D_A), tok),
            pl.BlockSpec((PROJ_TM, D_B), tok),
            pl.BlockSpec((None, D_A + D_B, D_MODEL), lambda i: (e, 0, 0)),
            pl.BlockSpec((None, 1, D_MODEL), lambda i: (ln_idx, 0, 0)),
            pl.BlockSpec((None, 1, D_MODEL), lambda i: (ln_idx, 0, 0)),
        ],
        out_specs=pl.BlockSpec((PROJ_TM, D_MODEL), tok),
        out_shape=jax.ShapeDtypeStruct((n, D_MODEL), F32),
        compiler_params=_params("parallel"),
        name="ab_out_ln",
    )(x, ya, yb, w_out, ln_g, ln_b)


def _conv_kernel(x_ref, xp_ref, xn_ref, win_ref, cw_ref, cb_ref, wout_ref, g_ref, b_ref, o_ref,
                 *, tiles_per_seq):
    i = pl.program_id(0)
    D = D_MODEL
    x = x_ref[...]
    xb = x.astype(BF16)
    xe = jnp.concatenate([xp_ref[...], x, xn_ref[...]], axis=0).astype(BF16)
    ue = _dot(xe, win_ref[:, D:2 * D]) * _dot(xe, win_ref[:, 2 * D:])
    ext = PROJ_TM + 2 * SUBLANES
    tile = slice(SUBLANES, SUBLANES + PROJ_TM)
    u = ue[tile]
    u_m1 = pltpu.roll(ue, 1, axis=0)[tile]
    u_p1 = pltpu.roll(ue, ext - 1, axis=0)[tile]
    pos = i % tiles_per_seq
    row = lax.broadcasted_iota(jnp.int32, (PROJ_TM, 1), 0)
    u_m1 = jnp.where(jnp.logical_and(row == 0, pos == 0), 0.0, u_m1)
    u_p1 = jnp.where(jnp.logical_and(row == PROJ_TM - 1, pos == tiles_per_seq - 1), 0.0, u_p1)
    y = cw_ref[0:1, :] * u_m1 + cw_ref[1:2, :] * u + cw_ref[2:3, :] * u_p1 + cb_ref[...]
    z = (_dot(xb, win_ref[:, :D]) * y).astype(BF16)
    mix = _dot(z, wout_ref[...])
    o_ref[...] = _layer_norm(DEEPNORM_ALPHA * x + mix, g_ref[...], b_ref[...])


def _conv_mixer_ln(x, w_in, conv_w, conv_b, w_out, ln_g, ln_b, o, ln_idx, seq):
    n = x.shape[0]
    tiles_per_seq = seq // PROJ_TM
    rb = PROJ_TM // SUBLANES
    last_rb = n // SUBLANES - 1
    tok = lambda i: (i, 0)
    return pl.pallas_call(
        functools.partial(_conv_kernel, tiles_per_seq=tiles_per_seq),
        grid=(n // PROJ_TM,),
        in_specs=[
            pl.BlockSpec((PROJ_TM, D_MODEL), tok),
            pl.BlockSpec((SUBLANES, D_MODEL), lambda i: (jnp.maximum(i * rb - 1, 0), 0)),
            pl.BlockSpec((SUBLANES, D_MODEL), lambda i: (jnp.minimum((i + 1) * rb, last_rb), 0)),
            pl.BlockSpec((None, D_MODEL, 3 * D_MODEL), lambda i: (o, 0, 0)),
            pl.BlockSpec((None, 3, D_MODEL), lambda i: (o, 0, 0)),
            pl.BlockSpec((None, 1, D_MODEL), lambda i: (o, 0, 0)),
            pl.BlockSpec((None, D_MODEL, D_MODEL), lambda i: (o, 0, 0)),
            pl.BlockSpec((None, 1, D_MODEL), lambda i: (ln_idx, 0, 0)),
            pl.BlockSpec((None, 1, D_MODEL), lambda i: (ln_idx, 0, 0)),
        ],
        out_specs=pl.BlockSpec((PROJ_TM, D_MODEL), tok),
        out_shape=jax.ShapeDtypeStruct((n, D_MODEL), F32),
        compiler_params=_params("parallel"),
        name="conv_mixer_ln",
    )(x, x, x, w_in, conv_w, conv_b, w_out, ln_g, ln_b)


def _prep_ab_weights(ab_w_in, ab_gate_b):
    n_even = ab_w_in.shape[0]
    w = ab_w_in
    cut = lambda lo, hi: w[:, :, lo:hi]
    qa, ka, va = cut(0, D_A), cut(D_A, 2 * D_A), cut(2 * D_A, 3 * D_A)
    o0 = 3 * D_A
    qb, kb, vb, ob = (cut(o0, o0 + D_B), cut(o0 + D_B, o0 + 2 * D_B),
                      cut(o0 + 2 * D_B, o0 + 3 * D_B), cut(o0 + 3 * D_B, o0 + 4 * D_B))
    gates = cut(o0 + 4 * D_B, o0 + 4 * D_B + 4 * ML_HEADS)
    gates = gates.reshape(n_even, D_MODEL, 4, ML_HEADS).transpose(0, 3, 2, 1)
    gates = jnp.pad(gates, ((0, 0), (0, 0), (0, SUBLANES - 4), (0, 0))).reshape(n_even, _GATE_ROWS, D_MODEL)
    wnn = jnp.concatenate([qa, va, qb, vb, ob], axis=2).astype(BF16)
    wnt = jnp.concatenate([ka.transpose(0, 2, 1), kb.transpose(0, 2, 1), gates], axis=1).astype(BF16)
    gb = jnp.pad(ab_gate_b.astype(F32).transpose(0, 2, 1), ((0, 0), (0, 0), (0, SUBLANES - 4)))
    return wnn, wnt, gb.reshape(n_even, _GATE_ROWS, 1)


def kernel(x, ln_g, ln_b, ffn_w_in, ffn_w_out, ab_w_in, ab_gate_b, na_rpb, ml_gn_g, ab_w_out,
           sc_w_in, sc_conv_w, sc_conv_b, sc_w_out):
    batch, seq, d = x.shape
    rows = seq // GRID_W
    h = x.reshape(batch * seq, d)
    ffn_in = ffn_w_in.astype(BF16)
    ffn_out = ffn_w_out.astype(BF16)
    lng = ln_g.astype(F32).reshape(DEPTH * 3, 1, d)
    lnb = ln_b.astype(F32).reshape(DEPTH * 3, 1, d)
    wnn, wnt, gate_b = _prep_ab_weights(ab_w_in, ab_gate_b)
    ab_out = ab_w_out.astype(BF16)
    gn = ml_gn_g.astype(F32).reshape(-1, 1, ML_HEAD_DIM)
    sc_in = sc_w_in.astype(BF16)
    sc_out = sc_w_out.astype(BF16)
    conv_b = sc_conv_b.astype(F32).reshape(-1, 1, d)
    conv_w = sc_conv_w.astype(F32)
    na_bias = _na_bias_tables(na_rpb.astype(F32), rows)
    na_bias = na_bias.reshape(-1, 2, *na_bias.shape[2:])

    for layer in range(DEPTH):
        h = _ffn_ln(h, ffn_in, ffn_out, lng, lnb, layer, 0, layer * 3)
        if layer % 2 == 0:
            e = layer // 2
            qa, va, qb, vb, ob, kat, kbt, gt = _ab_proj(h, wnn, wnt, gate_b, e, batch, seq)
            ya = _na_attention(qa, kat, va, na_bias, e, batch, seq)
            yb = _mlstm(qb, kbt, vb, ob, gt, gn, e, batch, seq)
            h = _ab_out_ln(h, ya, yb, ab_out, lng, lnb, e, layer * 3 + 1)
        else:
            o = layer // 2
            h = _conv_mixer_ln(h, sc_in, conv_w, conv_b, sc_out, lng, lnb, o, layer * 3 + 1, seq)
        h = _ffn_ln(h, ffn_in, ffn_out, lng, lnb, layer, 1, layer * 3 + 2)
    return h.reshape(batch, seq, d)
```

```python
import functools

import numpy as np
import jax
import jax.numpy as jnp
from jax import lax
from jax.experimental import pallas as pl
from jax.experimental.pallas import tpu as pltpu

D_MODEL = 1024
DEPTH = 4
GRID_W = 64
NA_HEAD_DIM = 64
NA_HEADS = 8
NA_WIN_H = 8
NA_WIN_W = 16
ML_HEADS = 4
ML_HEAD_DIM = 128
D_A = NA_HEADS * NA_HEAD_DIM
D_B = ML_HEADS * ML_HEAD_DIM
D_FF = 2816
DEEPNORM_ALPHA = (2 * DEPTH) ** 0.25
LN_EPS = 1e-5

LANES = 128
SUBLANES = 8
VMEM_LIMIT = 52 * 1024 * 1024

FFN_TM = 1024
FFN_TF = 256
PROJ_TM = 1024
ML_CHUNK = 256
ML_HEADS_PER_STEP = 2
NA_ROWS_PER_BLOCK = 2
NA_Q = NA_ROWS_PER_BLOCK * GRID_W
NA_KROWS = 10
NA_K = NA_KROWS * GRID_W
NA_MASKED = -1e30
NA_UNROLL = 1

BF16 = jnp.bfloat16
F32 = jnp.float32


def _dot(a, b):
    return jnp.dot(a, b, preferred_element_type=F32)


def _layer_norm(y, g, b):
    mu = jnp.mean(y, axis=-1, keepdims=True)
    yc = y - mu
    var = jnp.mean(yc * yc, axis=-1, keepdims=True)
    return yc * lax.rsqrt(var + LN_EPS) * g + b


def _params(*sem):
    return pltpu.CompilerParams(dimension_semantics=sem, vmem_limit_bytes=VMEM_LIMIT)


def _ffn_kernel(x_ref, wg_ref, wu_ref, wo_ref, g_ref, b_ref, o_ref):
    x = x_ref[...]
    xb = x.astype(BF16)
    acc = None
    for c in range(D_FF // FFN_TF):
        cols = slice(c * FFN_TF, (c + 1) * FFN_TF)
        gate = _dot(xb, wg_ref[:, cols])
        up = _dot(xb, wu_ref[:, cols])
        act = (gate * jax.nn.sigmoid(gate) * up).astype(BF16)
        part = _dot(act, wo_ref[cols, :])
        acc = part if acc is None else acc + part
    o_ref[...] = _layer_norm(DEEPNORM_ALPHA * x + 0.5 * acc, g_ref[...], b_ref[...])


def _resident(block_shape, index_map):
    return pl.BlockSpec(block_shape, index_map, pipeline_mode=pl.Buffered(1))


def _ffn_ln(x, w_in, w_out, ln_g, ln_b, layer, k, ln_idx):
    n = x.shape[0]
    return pl.pallas_call(
        _ffn_kernel,
        grid=(n // FFN_TM,),
        in_specs=[
            pl.BlockSpec((FFN_TM, D_MODEL), lambda i: (i, 0)),
            _resident((None, None, D_MODEL, D_FF), lambda i: (layer, k, 0, 0)),
            _resident((None, None, D_MODEL, D_FF), lambda i: (layer, k, 0, 1)),
            _resident((None, None, D_FF, D_MODEL), lambda i: (layer, k, 0, 0)),
            pl.BlockSpec((None, 1, D_MODEL), lambda i: (ln_idx, 0, 0)),
            pl.BlockSpec((None, 1, D_MODEL), lambda i: (ln_idx, 0, 0)),
        ],
        out_specs=pl.BlockSpec((FFN_TM, D_MODEL), lambda i: (i, 0)),
        out_shape=jax.ShapeDtypeStruct((n, D_MODEL), F32),
        compiler_params=_params("parallel"),
        name="ffn_ln",
    )(x, w_in, w_in, w_out, ln_g, ln_b)


_NN_GROUPS = 5
_GATE_ROWS = ML_HEADS * SUBLANES


def _ab_proj_kernel(x_ref, wnn_ref, wnt_ref, gb_ref,
                    qa_ref, va_ref, qb_ref, vb_ref, ob_ref, kat_ref, kbt_ref, gt_ref):
    xb = x_ref[...].astype(BF16)

    def nn(g):
        return _dot(xb, wnn_ref[:, g * D_A:(g + 1) * D_A])

    def nt(lo, hi):
        return lax.dot_general(wnt_ref[lo:hi, :], xb, (((1,), (1,)), ((), ())),
                               preferred_element_type=F32)

    qa_ref[...] = (nn(0) * (NA_HEAD_DIM ** -0.5)).astype(BF16)
    va_ref[...] = nn(1).astype(BF16)
    qb_ref[...] = nn(2).astype(BF16)
    vb_ref[...] = nn(3).astype(BF16)
    ob_ref[...] = nn(4)
    kat = nt(0, D_A).astype(BF16)
    kbt = nt(D_A, D_A + D_B).astype(BF16)
    gt = nt(D_A + D_B, D_A + D_B + _GATE_ROWS) + gb_ref[...]
    for c in range(PROJ_TM // LANES):
        kat_ref[c] = kat[:, c * LANES:(c + 1) * LANES]
        kbt_ref[c] = kbt[:, c * LANES:(c + 1) * LANES]
        gt_ref[c] = gt[:, c * LANES:(c + 1) * LANES]


def _ab_proj(x, wnn, wnt, gate_b, e, batch, seq):
    n = x.shape[0]
    tiles_per_seq = seq // PROJ_TM
    cpt = PROJ_TM // LANES
    nblk = seq // LANES
    nt_rows = D_A + D_B + _GATE_ROWS
    tok = lambda i: (i, 0)
    tr = lambda i: (i // tiles_per_seq, i % tiles_per_seq, 0, 0)
    nat = lambda dt: jax.ShapeDtypeStruct((n, D_A), dt)
    return pl.pallas_call(
        _ab_proj_kernel,
        grid=(n // PROJ_TM,),
        in_specs=[
            pl.BlockSpec((PROJ_TM, D_MODEL), tok),
            _resident((None, D_MODEL, _NN_GROUPS * D_A), lambda i: (e, 0, 0)),
            _resident((None, nt_rows, D_MODEL), lambda i: (e, 0, 0)),
            pl.BlockSpec((None, _GATE_ROWS, 1), lambda i: (e, 0, 0)),
        ],
        out_specs=[
            pl.BlockSpec((PROJ_TM, D_A), tok),
            pl.BlockSpec((PROJ_TM, D_A), tok),
            pl.BlockSpec((PROJ_TM, D_A), tok),
            pl.BlockSpec((PROJ_TM, D_A), tok),
            pl.BlockSpec((PROJ_TM, D_A), tok),
            pl.BlockSpec((None, cpt, D_A, LANES), tr),
            pl.BlockSpec((None, cpt, D_B, LANES), tr),
            pl.BlockSpec((None, cpt, _GATE_ROWS, LANES), tr),
        ],
        out_shape=[
            nat(BF16), nat(BF16), nat(BF16), nat(BF16), nat(F32),
            jax.ShapeDtypeStruct((batch, nblk, D_A, LANES), BF16),
            jax.ShapeDtypeStruct((batch, nblk, D_B, LANES), BF16),
            jax.ShapeDtypeStruct((batch, nblk, _GATE_ROWS, LANES), F32),
        ],
        compiler_params=_params("parallel"),
        name="ab_proj",
    )(x, wnn, wnt, gate_b)


def _na_block_window_start(r0, rows):
    return np.clip(r0 - NA_WIN_H // 2, 0, rows - NA_KROWS)


def _na_block_classes(rows):
    reps = []
    cls_of_block = []
    seen = {}
    wh = min(NA_WIN_H, rows)
    for blk in range(rows // NA_ROWS_PER_BLOCK):
        r0 = blk * NA_ROWS_PER_BLOCK
        ws = _na_block_window_start(r0, rows)
        key = tuple(int(np.clip(r0 + d - wh // 2, 0, rows - wh)) - int(ws) for d in range(NA_ROWS_PER_BLOCK)) \
            + tuple(r0 + d - int(ws) for d in range(NA_ROWS_PER_BLOCK))
        if key not in seen:
            seen[key] = len(reps)
            reps.append(r0)
        cls_of_block.append(seen[key])
    return reps, cls_of_block


def _na_bias_tables(rpb, rows):
    reps, _ = _na_block_classes(rows)
    wh = min(NA_WIN_H, rows)
    n_dc = 2 * NA_WIN_W - 1
    col = np.arange(GRID_W)
    dc = np.clip(col[None, :] - col[:, None] + (NA_WIN_W - 1), 0, n_dc - 1)
    cs = np.clip(col - NA_WIN_W // 2, 0, GRID_W - NA_WIN_W)
    ok_c = (col[None, :] >= cs[:, None]) & (col[None, :] < cs[:, None] + NA_WIN_W)
    sel_dc = (dc[None] == np.arange(n_dc)[:, None, None]).astype(np.float32)
    by_col = jnp.einsum('...rd,dqk->...rqk', rpb, sel_dc, precision=lax.Precision.HIGHEST)
    by_col = jnp.where(ok_c, by_col, NA_MASKED).astype(F32)
    masked = jnp.full(by_col.shape[:-3] + (GRID_W, GRID_W), NA_MASKED, F32)
    tables = []
    for r0 in reps:
        ws = int(_na_block_window_start(r0, rows))
        q_rows = []
        for dq in range(NA_ROWS_PER_BLOCK):
            qr = r0 + dq
            rs = int(np.clip(qr - wh // 2, 0, rows - wh))
            slabs = [by_col[..., ws + i - qr + (NA_WIN_H - 1), :, :] if rs <= ws + i < rs + wh else masked
                     for i in range(NA_KROWS)]
            q_rows.append(jnp.concatenate(slabs, axis=-1))
        tables.append(jnp.concatenate(q_rows, axis=-2))
    return jnp.stack(tables, axis=-3)


def _na_kernel(q_ref, kt_ref, v_ref, bias_ref, o_ref, *, rows, cls_of_block):
    lane = lax.broadcasted_iota(jnp.int32, (NA_Q, LANES), 1)
    first_head = lane < NA_HEAD_DIM
    kblocks = NA_K // LANES

    def window_start(blk):
        return int(_na_block_window_start(blk * NA_ROWS_PER_BLOCK, rows))

    def score_stage(group):
        scores = []
        for blk in group:
            q2 = q_ref[blk * NA_Q:(blk + 1) * NA_Q, :]
            kb0 = window_start(blk) * GRID_W // LANES
            kwin = jnp.concatenate([kt_ref[kb0 + c] for c in range(kblocks)], axis=1)
            for h in range(2):
                qh = jnp.where(first_head if h == 0 else jnp.logical_not(first_head), q2, jnp.zeros_like(q2))
                scores.append(_dot(qh, kwin) + bias_ref[h, cls_of_block[blk]])
        return scores

    def output_stage(group, scores):
        probs = []
        for s in scores:
            p = jnp.exp(s - jnp.max(s, axis=-1, keepdims=True))
            probs.append((p.astype(BF16), jnp.sum(p, axis=-1, keepdims=True)))
        for u, blk in enumerate(group):
            k0 = window_start(blk) * GRID_W
            vwin = v_ref[k0:k0 + NA_K, :]
            outs = [_dot(p, vwin) / l for p, l in probs[2 * u:2 * u + 2]]
            o_ref[blk * NA_Q:(blk + 1) * NA_Q, :] = jnp.where(first_head, outs[0], outs[1]).astype(BF16)

    nblk = rows // NA_ROWS_PER_BLOCK
    groups = [range(g, g + NA_UNROLL) for g in range(0, nblk, NA_UNROLL)]
    scores = score_stage(groups[0])
    for g, group in enumerate(groups):
        ahead = score_stage(groups[g + 1]) if g + 1 < len(groups) else None
        output_stage(group, scores)
        scores = ahead


def _na_attention(qa, kat, va, bias, e, batch, seq):
    n = qa.shape[0]
    rows = seq // GRID_W
    _, cls_of_block = _na_block_classes(rows)
    ncls = bias.shape[2]
    pairs = NA_HEADS // 2
    return pl.pallas_call(
        functools.partial(_na_kernel, rows=rows, cls_of_block=tuple(cls_of_block)),
        grid=(pairs, batch),
        in_specs=[
            pl.BlockSpec((seq, LANES), lambda p, b: (b, p)),
            pl.BlockSpec((None, seq // LANES, LANES, LANES), lambda p, b: (b, 0, p, 0)),
            pl.BlockSpec((seq, LANES), lambda p, b: (b, p)),
            pl.BlockSpec((None, 2, ncls, NA_Q, NA_K), lambda p, b: (e * pairs + p, 0, 0, 0, 0)),
        ],
        out_specs=pl.BlockSpec((seq, LANES), lambda p, b: (b, p)),
        out_shape=jax.ShapeDtypeStruct((n, D_A), BF16),
        compiler_params=_params("parallel", "parallel"),
        name="na_attention",
    )(qa, kat, va, bias)


def _log_sigmoid(x):
    return jnp.minimum(x, 0.0) - jnp.log1p(jnp.exp(-jnp.abs(x)))


def _mlstm_kernel(q_ref, kt_ref, v_ref, ob_ref, gt_ref, gn_ref, o_ref,
                  h_ref, src_ref, srcmax_ref, gtot_ref, bcols_ref, *, seq):
    L = ML_CHUNK
    nc = seq // L
    assert nc % 2 == 0
    cb = L // LANES
    log_scale = -0.5 * float(np.log(ML_HEAD_DIM))
    row_i = lax.broadcasted_iota(jnp.int32, (L, L), 0)
    col_i = lax.broadcasted_iota(jnp.int32, (L, L), 1)
    ones_col = (lax.broadcasted_iota(jnp.int32, (L, LANES), 1) == 0).astype(BF16)

    def tri_mask(reverse):
        return (col_i >= row_i) if reverse else (col_i <= row_i)

    def stat_row(c, head, reverse):
        return (head * nc + c) * SUBLANES + (2 if reverse else 0)

    def gate_prologue(head):
        hrows = slice(head * SUBLANES, (head + 1) * SUBLANES)
        x = jnp.concatenate(
            [jnp.concatenate([gt_ref[c * cb + j, hrows, :] for j in range(cb)], axis=1) for c in range(nc)],
            axis=0)
        n = nc * SUBLANES
        kind = lax.broadcasted_iota(jnp.int32, (n, 1), 0) % SUBLANES
        logf = _log_sigmoid(x)

        def split3(a):
            hi = a.astype(BF16)
            rest = a - hi.astype(F32)
            mid = rest.astype(BF16)
            return hi, mid, (rest - mid.astype(F32)).astype(BF16)

        pieces = split3(logf)
        pad = jnp.zeros((LANES - n, L), F32)
        pieces_t = split3(jnp.concatenate([logf, pad], axis=0).T)
        lower = tri_mask(False).astype(BF16)
        upper = tri_mask(True).astype(BF16)
        b_rows_f = sum(_dot(p, upper) for p in pieces)
        b_rows_r = sum(_dot(p, lower) for p in pieces)
        b_cols_f = sum(_dot(lower, p) for p in pieces_t)
        b_cols_r = sum(_dot(upper, p) for p in pieces_t)
        up1 = lambda a: pltpu.roll(a, n - 1, axis=0)
        src = x - jnp.where(kind == 0, up1(b_rows_f), up1(b_rows_r))
        rows = pl.ds(head * n, n)
        src_ref[rows, :] = src
        srcmax_ref[rows, :] = jnp.broadcast_to(jnp.max(src, axis=1, keepdims=True), (n, L))
        gtot_ref[rows, :] = up1(jnp.broadcast_to(jnp.sum(logf, axis=1, keepdims=True), (n, L)))
        bcols_ref[2 * head] = b_cols_f
        bcols_ref[2 * head + 1] = b_cols_r

    def stage_load(c, carry, head, reverse):
        rows = slice(c * L, (c + 1) * L)
        lanes = slice(head * ML_HEAD_DIM, (head + 1) * ML_HEAD_DIM)
        b_lane = c * SUBLANES + (3 if reverse else 1)
        st = dict(
            tri=tri_mask(reverse), row=stat_row(c, head, reverse),
            b_col=bcols_ref[2 * head + int(reverse), :, b_lane:b_lane + 1],
            q=q_ref[rows, lanes],
            kt=jnp.concatenate([kt_ref[c * cb + j, lanes, :] for j in range(cb)], axis=1),
            v_ext=jnp.concatenate([v_ref[rows, lanes], ones_col], axis=1),
            c_ext=carry[0], m=carry[1])
        st['qk'] = _dot(st['q'], st['kt'])
        st['qc'] = _dot(st['q'], st['c_ext'].astype(BF16))
        return st

    def stage_gates(st):
        m, slot = st['m'], pl.ds(st['row'], 1)
        src = src_ref[slot, :]
        mm = jnp.maximum(m, srcmax_ref[slot, :])
        st['w_src'] = jnp.exp(src + (log_scale - mm))
        st['w_carry'] = jnp.exp(m - mm)[:, :1]
        st['floor'] = jnp.exp(-(st['b_col'] + mm[:, :1]))
        st['m_new'] = gtot_ref[slot, :] + mm

    def stage_state(st):
        kv = _dot((st['kt'].astype(F32) * st['w_src']).astype(BF16), st['v_ext'])
        return st['w_carry'] * st['c_ext'] + kv, st['m_new']

    def stage_out(st):
        qkw = jnp.where(st['tri'], st['qk'] * st['w_src'], 0.0)
        tot = _dot(qkw.astype(BF16), st['v_ext']) + st['w_carry'] * st['qc']
        num = tot[:, :ML_HEAD_DIM]
        den = tot[:, ML_HEAD_DIM:ML_HEAD_DIM + 1]
        return num / jnp.maximum(jnp.abs(den), st['floor'])

    def finish(c, head, h):
        rows = slice(c * L, (c + 1) * L)
        lanes = slice(head * ML_HEAD_DIM, (head + 1) * ML_HEAD_DIM)
        mu = jnp.mean(h, axis=-1, keepdims=True)
        hc = h - mu
        var = jnp.mean(hc * hc, axis=-1, keepdims=True)
        hn = hc * lax.rsqrt(var + LN_EPS)
        gate = jax.nn.sigmoid(ob_ref[rows, lanes])
        o_ref[rows, lanes] = (gate * hn * gn_ref[head]).astype(BF16)

    def body(ci, carry, second_half):
        streams = []
        for head in range(ML_HEADS_PER_STEP):
            streams.append((ci, head, stage_load(ci, carry[2 * head], head, False)))
            streams.append((nc - 1 - ci, head, stage_load(nc - 1 - ci, carry[2 * head + 1], head, True)))
        for _, _, st in streams:
            stage_gates(st)
        new_carry = tuple(stage_state(st) for _, _, st in streams)
        for c, head, st in streams:
            h = stage_out(st)
            rows = slice(c * L, (c + 1) * L)
            lanes = slice(head * ML_HEAD_DIM, (head + 1) * ML_HEAD_DIM)
            if second_half:
                finish(c, head, h + h_ref[rows, lanes])
            else:
                h_ref[rows, lanes] = h
        return new_carry

    for head in range(ML_HEADS_PER_STEP):
        gate_prologue(head)

    zero = (jnp.zeros((ML_HEAD_DIM, 2 * LANES), F32), jnp.zeros((1, L), F32))
    carry = (zero,) * (2 * ML_HEADS_PER_STEP)
    for ci in range(nc):
        carry = body(ci, carry, second_half=ci >= nc // 2)


def _mlstm(qb, kbt, vb, ob, gt, gn_g, e, batch, seq):
    n = qb.shape[0]
    hps = ML_HEADS_PER_STEP
    width = hps * ML_HEAD_DIM
    tok = lambda b, g: (b, g)
    return pl.pallas_call(
        functools.partial(_mlstm_kernel, seq=seq),
        grid=(batch, ML_HEADS // hps),
        in_specs=[
            pl.BlockSpec((seq, width), tok),
            pl.BlockSpec((None, seq // LANES, width, LANES), lambda b, g: (b, 0, g, 0)),
            pl.BlockSpec((seq, width), tok),
            pl.BlockSpec((seq, width), tok),
            pl.BlockSpec((None, seq // LANES, hps * SUBLANES, LANES), lambda b, g: (b, 0, g, 0)),
            pl.BlockSpec((hps, 1, ML_HEAD_DIM), lambda b, g: (e * (ML_HEADS // hps) + g, 0, 0)),
        ],
        out_specs=pl.BlockSpec((seq, width), tok),
        out_shape=jax.ShapeDtypeStruct((n, D_B), BF16),
        scratch_shapes=[pltpu.VMEM((seq, width), F32)]
        + [pltpu.VMEM((hps * (seq // ML_CHUNK) * SUBLANES, ML_CHUNK), F32)] * 3
        + [pltpu.VMEM((2 * hps, ML_CHUNK, LANES), F32)],
        compiler_params=_params("parallel", "parallel"),
        name="mlstm",
    )(qb, kbt, vb, ob, gt, gn_g)


def _ab_out_kernel(x_ref, ya_ref, yb_ref, w_ref, g_ref, b_ref, o_ref):
    mix = _dot(ya_ref[...], w_ref[:D_A, :]) + _dot(yb_ref[...], w_ref[D_A:, :])
    o_ref[...] = _layer_norm(DEEPNORM_ALPHA * x_ref[...] + mix, g_ref[...], b_ref[...])


def _ab_out_ln(x, ya, yb, w_out, ln_g, ln_b, e, ln_idx):
    n = x.shape[0]
    tok = lambda i: (i, 0)
    return pl.pallas_call(
        _ab_out_kernel,
        grid=(n // PROJ_TM,),
        in_specs=[
            pl.BlockSpec((PROJ_TM, D_MODEL), tok),
            pl.BlockSpec((PROJ_TM, D_A), tok),
            pl.BlockSpec((PROJ_TM, D_B), tok),
            _resident((None, D_A + D_B, D_MODEL), lambda i: (e, 0, 0)),
            pl.BlockSpec((None, 1, D_MODEL), lambda i: (ln_idx, 0, 0)),
            pl.BlockSpec((None, 1, D_MODEL), lambda i: (ln_idx, 0, 0)),
        ],
        out_specs=pl.BlockSpec((PROJ_TM, D_MODEL), tok),
        out_shape=jax.ShapeDtypeStruct((n, D_MODEL), F32),
        compiler_params=_params("parallel"),
        name="ab_out_ln",
    )(x, ya, yb, w_out, ln_g, ln_b)


def _conv_kernel(x_ref, xp_ref, xn_ref, win_ref, cw_ref, cb_ref, wout_ref, g_ref, b_ref, o_ref,
                 *, tiles_per_seq):
    i = pl.program_id(0)
    D = D_MODEL
    x = x_ref[...]
    xb = x.astype(BF16)
    xe = jnp.concatenate([xp_ref[...], x, xn_ref[...]], axis=0).astype(BF16)
    ue = _dot(xe, win_ref[:, D:2 * D]) * _dot(xe, win_ref[:, 2 * D:])
    ext = PROJ_TM + 2 * SUBLANES
    tile = slice(SUBLANES, SUBLANES + PROJ_TM)
    u = ue[tile]
    u_m1 = pltpu.roll(ue, 1, axis=0)[tile]
    u_p1 = pltpu.roll(ue, ext - 1, axis=0)[tile]
    pos = i % tiles_per_seq
    row = lax.broadcasted_iota(jnp.int32, (PROJ_TM, 1), 0)
    u_m1 = jnp.where(jnp.logical_and(row == 0, pos == 0), 0.0, u_m1)
    u_p1 = jnp.where(jnp.logical_and(row == PROJ_TM - 1, pos == tiles_per_seq - 1), 0.0, u_p1)
    y = cw_ref[0:1, :] * u_m1 + cw_ref[1:2, :] * u + cw_ref[2:3, :] * u_p1 + cb_ref[...]
    z = (_dot(xb, win_ref[:, :D]) * y).astype(BF16)
    mix = _dot(z, wout_ref[...])
    o_ref[...] = _layer_norm(DEEPNORM_ALPHA * x + mix, g_ref[...], b_ref[...])


def _conv_mixer_ln(x, w_in, conv_w, conv_b, w_out, ln_g, ln_b, o, ln_idx, seq):
    n = x.shape[0]
    tiles_per_seq = seq // PROJ_TM
    rb = PROJ_TM // SUBLANES
    last_rb = n // SUBLANES - 1
    tok = lambda i: (i, 0)
    return pl.pallas_call(
        functools.partial(_conv_kernel, tiles_per_seq=tiles_per_seq),
        grid=(n // PROJ_TM,),
        in_specs=[
            pl.BlockSpec((PROJ_TM, D_MODEL), tok),
            pl.BlockSpec((SUBLANES, D_MODEL), lambda i: (jnp.maximum(i * rb - 1, 0), 0)),
            pl.BlockSpec((SUBLANES, D_MODEL), lambda i: (jnp.minimum((i + 1) * rb, last_rb), 0)),
            _resident((None, D_MODEL, 3 * D_MODEL), lambda i: (o, 0, 0)),
            pl.BlockSpec((None, 3, D_MODEL), lambda i: (o, 0, 0)),
            pl.BlockSpec((None, 1, D_MODEL), lambda i: (o, 0, 0)),
            _resident((None, D_MODEL, D_MODEL), lambda i: (o, 0, 0)),
            pl.BlockSpec((None, 1, D_MODEL), lambda i: (ln_idx, 0, 0)),
            pl.BlockSpec((None, 1, D_MODEL), lambda i: (ln_idx, 0, 0)),
        ],
        out_specs=pl.BlockSpec((PROJ_TM, D_MODEL), tok),
        out_shape=jax.ShapeDtypeStruct((n, D_MODEL), F32),
        compiler_params=_params("parallel"),
        name="conv_mixer_ln",
    )(x, x, x, w_in, conv_w, conv_b, w_out, ln_g, ln_b)


def _prep_ab_weights(ab_w_in, ab_gate_b):
    n_even = ab_w_in.shape[0]
    w = ab_w_in
    cut = lambda lo, hi: w[:, :, lo:hi]
    qa, ka, va = cut(0, D_A), cut(D_A, 2 * D_A), cut(2 * D_A, 3 * D_A)
    o0 = 3 * D_A
    qb, kb, vb, ob = (cut(o0, o0 + D_B), cut(o0 + D_B, o0 + 2 * D_B),
                      cut(o0 + 2 * D_B, o0 + 3 * D_B), cut(o0 + 3 * D_B, o0 + 4 * D_B))
    gates = cut(o0 + 4 * D_B, o0 + 4 * D_B + 4 * ML_HEADS)
    gates = gates.reshape(n_even, D_MODEL, 4, ML_HEADS).transpose(0, 3, 2, 1)
    gates = jnp.pad(gates, ((0, 0), (0, 0), (0, SUBLANES - 4), (0, 0))).reshape(n_even, _GATE_ROWS, D_MODEL)
    wnn = jnp.concatenate([qa, va, qb, vb, ob], axis=2).astype(BF16)
    wnt = jnp.concatenate([ka.transpose(0, 2, 1), kb.transpose(0, 2, 1), gates], axis=1).astype(BF16)
    gb = jnp.pad(ab_gate_b.astype(F32).transpose(0, 2, 1), ((0, 0), (0, 0), (0, SUBLANES - 4)))
    return wnn, wnt, gb.reshape(n_even, _GATE_ROWS, 1)


def kernel(x, ln_g, ln_b, ffn_w_in, ffn_w_out, ab_w_in, ab_gate_b, na_rpb, ml_gn_g, ab_w_out,
           sc_w_in, sc_conv_w, sc_conv_b, sc_w_out):
    batch, seq, d = x.shape
    rows = seq // GRID_W
    h = x.reshape(batch * seq, d)
    ffn_in = ffn_w_in.astype(BF16)
    ffn_out = ffn_w_out.astype(BF16)
    lng = ln_g.astype(F32).reshape(DEPTH * 3, 1, d)
    lnb = ln_b.astype(F32).reshape(DEPTH * 3, 1, d)
    wnn, wnt, gate_b = _prep_ab_weights(ab_w_in, ab_gate_b)
    ab_out = ab_w_out.astype(BF16)
    gn = ml_gn_g.astype(F32).reshape(-1, 1, ML_HEAD_DIM)
    sc_in = sc_w_in.astype(BF16)
    sc_out = sc_w_out.astype(BF16)
    conv_b = sc_conv_b.astype(F32).reshape(-1, 1, d)
    conv_w = sc_conv_w.astype(F32)
    na_bias = _na_bias_tables(na_rpb.astype(F32), rows)
    na_bias = na_bias.reshape(-1, 2, *na_bias.shape[2:])

    for layer in range(DEPTH):
        h = _ffn_ln(h, ffn_in, ffn_out, lng, lnb, layer, 0, layer * 3)
        if layer % 2 == 0:
            e = layer // 2
            qa, va, qb, vb, ob, kat, kbt, gt = _ab_proj(h, wnn, wnt, gate_b, e, batch, seq)
            ya = _na_attention(qa, kat, va, na_bias, e, batch, seq)
            yb = _mlstm(qb, kbt, vb, ob, gt, gn, e, batch, seq)
            h = _ab_out_ln(h, ya, yb, ab_out, lng, lnb, e, layer * 3 + 1)
        else:
            o = layer // 2
            h = _conv_mixer_ln(h, sc_in, conv_w, conv_b, sc_out, lng, lnb, o, layer * 3 + 1, seq)
        h = _ffn_ln(h, ffn_in, ffn_out, lng, lnb, layer, 1, layer * 3 + 2)
    return h.reshape(batch, seq, d)
```

```python
import functools

import numpy as np
import jax
import jax.numpy as jnp
from jax import lax
from jax.experimental import pallas as pl
from jax.experimental.pallas import tpu as pltpu

D_MODEL = 1024
DEPTH = 4
GRID_W = 64
NA_HEAD_DIM = 64
NA_HEADS = 8
NA_WIN_H = 8
NA_WIN_W = 16
ML_HEADS = 4
ML_HEAD_DIM = 128
D_A = NA_HEADS * NA_HEAD_DIM
D_B = ML_HEADS * ML_HEAD_DIM
D_FF = 2816
DEEPNORM_ALPHA = (2 * DEPTH) ** 0.25
LN_EPS = 1e-5

LANES = 128
SUBLANES = 8
VMEM_LIMIT = 52 * 1024 * 1024

FFN_TM = 1024
FFN_TF = 256
FFN_CAST_OUT_BLOCKS = 16
PROJ_TM = 1024
ML_CHUNK = 256
ML_HEADS_PER_STEP = 2
NA_ROWS_PER_BLOCK = 2
NA_Q = NA_ROWS_PER_BLOCK * GRID_W
NA_KROWS = 10
NA_K = NA_KROWS * GRID_W
NA_MASKED = -1e30
NA_UNROLL = 1

BF16 = jnp.bfloat16
F32 = jnp.float32


def _dot(a, b):
    return jnp.dot(a, b, preferred_element_type=F32)


def _layer_norm(y, g, b):
    mu = jnp.mean(y, axis=-1, keepdims=True)
    yc = y - mu
    var = jnp.mean(yc * yc, axis=-1, keepdims=True)
    return yc * lax.rsqrt(var + LN_EPS) * g + b


def _params(*sem):
    return pltpu.CompilerParams(dimension_semantics=sem, vmem_limit_bytes=VMEM_LIMIT)


def _ffn_kernel(x_ref, wg_ref, wu_ref, wo_ref, g_ref, b_ref, *rest):
    if len(rest) == 1:
        (o_ref,) = rest
    else:
        nwi_ref, nwo_ref, o_ref, nwi_out_ref, nwo_out_ref = rest
        nwi_out_ref[...] = nwi_ref[...].astype(BF16)
        nwo_out_ref[...] = nwo_ref[...].astype(BF16)
    x = x_ref[...]
    xb = x.astype(BF16)
    acc = None
    for c in range(D_FF // FFN_TF):
        cols = slice(c * FFN_TF, (c + 1) * FFN_TF)
        gate = _dot(xb, wg_ref[:, cols])
        up = _dot(xb, wu_ref[:, cols])
        act = (gate * jax.nn.sigmoid(gate) * up).astype(BF16)
        part = _dot(act, wo_ref[cols, :])
        acc = part if acc is None else acc + part
    o_ref[...] = _layer_norm(DEEPNORM_ALPHA * x + 0.5 * acc, g_ref[...], b_ref[...])


def _resident(block_shape, index_map):
    return pl.BlockSpec(block_shape, index_map, pipeline_mode=pl.Buffered(1))


def _ffn_ln(x, w_in, w_out, ln_g, ln_b, ln_idx, next_f32=None):
    n = x.shape[0]
    steps = n // FFN_TM
    in_specs = [
        pl.BlockSpec((FFN_TM, D_MODEL), lambda i: (i, 0)),
        _resident((D_MODEL, D_FF), lambda i: (0, 0)),
        _resident((D_MODEL, D_FF), lambda i: (0, 1)),
        _resident((D_FF, D_MODEL), lambda i: (0, 0)),
        pl.BlockSpec((None, 1, D_MODEL), lambda i: (ln_idx, 0, 0)),
        pl.BlockSpec((None, 1, D_MODEL), lambda i: (ln_idx, 0, 0)),
    ]
    out_specs = [pl.BlockSpec((FFN_TM, D_MODEL), lambda i: (i, 0))]
    out_shape = [jax.ShapeDtypeStruct((n, D_MODEL), F32)]
    operands = [x, w_in, w_in, w_out, ln_g, ln_b]
    if next_f32 is not None:
        nwi, nwo, layer, k = next_f32
        wi_rows = D_MODEL // steps
        wo_rows = D_FF // FFN_CAST_OUT_BLOCKS
        visits = steps // FFN_CAST_OUT_BLOCKS
        in_specs += [pl.BlockSpec((None, None, wi_rows, 2 * D_FF), lambda i: (layer, k, i, 0)),
                     pl.BlockSpec((None, None, wo_rows, D_MODEL), lambda i: (layer, k, i // visits, 0))]
        out_specs += [pl.BlockSpec((wi_rows, 2 * D_FF), lambda i: (i, 0)),
                      pl.BlockSpec((wo_rows, D_MODEL), lambda i: (i // visits, 0))]
        out_shape += [jax.ShapeDtypeStruct((D_MODEL, 2 * D_FF), BF16),
                      jax.ShapeDtypeStruct((D_FF, D_MODEL), BF16)]
        operands += [nwi, nwo]
    return pl.pallas_call(
        _ffn_kernel,
        grid=(steps,),
        in_specs=in_specs,
        out_specs=out_specs,
        out_shape=out_shape,
        compiler_params=_params("arbitrary"),
        name="ffn_ln",
    )(*operands)


_NN_GROUPS = 5
_GATE_ROWS = ML_HEADS * SUBLANES


def _ab_proj_kernel(x_ref, wnn_ref, wnt_ref, gb_ref,
                    qa_ref, va_ref, qb_ref, vb_ref, ob_ref, kat_ref, kbt_ref, gt_ref):
    xb = x_ref[...].astype(BF16)

    def nn(g):
        return _dot(xb, wnn_ref[:, g * D_A:(g + 1) * D_A])

    def nt(lo, hi):
        return lax.dot_general(wnt_ref[lo:hi, :], xb, (((1,), (1,)), ((), ())),
                               preferred_element_type=F32)

    qa_ref[...] = (nn(0) * (NA_HEAD_DIM ** -0.5)).astype(BF16)
    va_ref[...] = nn(1).astype(BF16)
    qb_ref[...] = nn(2).astype(BF16)
    vb_ref[...] = nn(3).astype(BF16)
    ob_ref[...] = nn(4)
    kat = nt(0, D_A).astype(BF16)
    kbt = nt(D_A, D_A + D_B).astype(BF16)
    gt = nt(D_A + D_B, D_A + D_B + _GATE_ROWS) + gb_ref[...]
    for c in range(PROJ_TM // LANES):
        kat_ref[c] = kat[:, c * LANES:(c + 1) * LANES]
        kbt_ref[c] = kbt[:, c * LANES:(c + 1) * LANES]
        gt_ref[c] = gt[:, c * LANES:(c + 1) * LANES]


def _ab_proj(x, wnn, wnt, gate_b, e, batch, seq):
    n = x.shape[0]
    tiles_per_seq = seq // PROJ_TM
    cpt = PROJ_TM // LANES
    nblk = seq // LANES
    nt_rows = D_A + D_B + _GATE_ROWS
    tok = lambda i: (i, 0)
    tr = lambda i: (i // tiles_per_seq, i % tiles_per_seq, 0, 0)
    nat = lambda dt: jax.ShapeDtypeStruct((n, D_A), dt)
    return pl.pallas_call(
        _ab_proj_kernel,
        grid=(n // PROJ_TM,),
        in_specs=[
            pl.BlockSpec((PROJ_TM, D_MODEL), tok),
            _resident((None, D_MODEL, _NN_GROUPS * D_A), lambda i: (e, 0, 0)),
            _resident((None, nt_rows, D_MODEL), lambda i: (e, 0, 0)),
            pl.BlockSpec((None, _GATE_ROWS, 1), lambda i: (e, 0, 0)),
        ],
        out_specs=[
            pl.BlockSpec((PROJ_TM, D_A), tok),
            pl.BlockSpec((PROJ_TM, D_A), tok),
            pl.BlockSpec((PROJ_TM, D_A), tok),
            pl.BlockSpec((PROJ_TM, D_A), tok),
            pl.BlockSpec((PROJ_TM, D_A), tok),
            pl.BlockSpec((None, cpt, D_A, LANES), tr),
            pl.BlockSpec((None, cpt, D_B, LANES), tr),
            pl.BlockSpec((None, cpt, _GATE_ROWS, LANES), tr),
        ],
        out_shape=[
            nat(BF16), nat(BF16), nat(BF16), nat(BF16), nat(F32),
            jax.ShapeDtypeStruct((batch, nblk, D_A, LANES), BF16),
            jax.ShapeDtypeStruct((batch, nblk, D_B, LANES), BF16),
            jax.ShapeDtypeStruct((batch, nblk, _GATE_ROWS, LANES), F32),
        ],
        compiler_params=_params("parallel"),
        name="ab_proj",
    )(x, wnn, wnt, gate_b)


def _na_block_window_start(r0, rows):
    return np.clip(r0 - NA_WIN_H // 2, 0, rows - NA_KROWS)


def _na_block_classes(rows):
    reps = []
    cls_of_block = []
    seen = {}
    wh = min(NA_WIN_H, rows)
    for blk in range(rows // NA_ROWS_PER_BLOCK):
        r0 = blk * NA_ROWS_PER_BLOCK
        ws = _na_block_window_start(r0, rows)
        key = tuple(int(np.clip(r0 + d - wh // 2, 0, rows - wh)) - int(ws) for d in range(NA_ROWS_PER_BLOCK)) \
            + tuple(r0 + d - int(ws) for d in range(NA_ROWS_PER_BLOCK))
        if key not in seen:
            seen[key] = len(reps)
            reps.append(r0)
        cls_of_block.append(seen[key])
    return reps, cls_of_block


def _na_bias_tables(rpb, rows):
    reps, _ = _na_block_classes(rows)
    wh = min(NA_WIN_H, rows)
    n_dc = 2 * NA_WIN_W - 1
    col = np.arange(GRID_W)
    dc = np.clip(col[None, :] - col[:, None] + (NA_WIN_W - 1), 0, n_dc - 1)
    cs = np.clip(col - NA_WIN_W // 2, 0, GRID_W - NA_WIN_W)
    ok_c = (col[None, :] >= cs[:, None]) & (col[None, :] < cs[:, None] + NA_WIN_W)
    sel_dc = (dc[None] == np.arange(n_dc)[:, None, None]).astype(np.float32)
    by_col = jnp.einsum('...rd,dqk->...rqk', rpb, sel_dc, precision=lax.Precision.HIGHEST)
    by_col = jnp.where(ok_c, by_col, NA_MASKED).astype(F32)
    masked = jnp.full(by_col.shape[:-3] + (GRID_W, GRID_W), NA_MASKED, F32)
    tables = []
    for r0 in reps:
        ws = int(_na_block_window_start(r0, rows))
        q_rows = []
        for dq in range(NA_ROWS_PER_BLOCK):
            qr = r0 + dq
            rs = int(np.clip(qr - wh // 2, 0, rows - wh))
            slabs = [by_col[..., ws + i - qr + (NA_WIN_H - 1), :, :] if rs <= ws + i < rs + wh else masked
                     for i in range(NA_KROWS)]
            q_rows.append(jnp.concatenate(slabs, axis=-1))
        tables.append(jnp.concatenate(q_rows, axis=-2))
    return jnp.stack(tables, axis=-3)


def _na_kernel(q_ref, kt_ref, v_ref, bias_ref, o_ref, *, rows, cls_of_block):
    lane = lax.broadcasted_iota(jnp.int32, (NA_Q, LANES), 1)
    first_head = lane < NA_HEAD_DIM
    kblocks = NA_K // LANES

    def window_start(blk):
        return int(_na_block_window_start(blk * NA_ROWS_PER_BLOCK, rows))

    def score_stage(group):
        scores = []
        for blk in group:
            q2 = q_ref[blk * NA_Q:(blk + 1) * NA_Q, :]
            kb0 = window_start(blk) * GRID_W // LANES
            kwin = jnp.concatenate([kt_ref[kb0 + c] for c in range(kblocks)], axis=1)
            for h in range(2):
                qh = jnp.where(first_head if h == 0 else jnp.logical_not(first_head), q2, jnp.zeros_like(q2))
                scores.append(_dot(qh, kwin) + bias_ref[h, cls_of_block[blk]])
        return scores

    def output_stage(group, scores):
        probs = []
        for s in scores:
            p = jnp.exp(s - jnp.max(s, axis=-1, keepdims=True))
            probs.append((p.astype(BF16), jnp.sum(p, axis=-1, keepdims=True)))
        for u, blk in enumerate(group):
            k0 = window_start(blk) * GRID_W
            vwin = v_ref[k0:k0 + NA_K, :]
            outs = [_dot(p, vwin) / l for p, l in probs[2 * u:2 * u + 2]]
            o_ref[blk * NA_Q:(blk + 1) * NA_Q, :] = jnp.where(first_head, outs[0], outs[1]).astype(BF16)

    nblk = rows // NA_ROWS_PER_BLOCK
    groups = [range(g, g + NA_UNROLL) for g in range(0, nblk, NA_UNROLL)]
    scores = score_stage(groups[0])
    for g, group in enumerate(groups):
        ahead = score_stage(groups[g + 1]) if g + 1 < len(groups) else None
        output_stage(group, scores)
        scores = ahead


def _na_attention(qa, kat, va, bias, e, batch, seq):
    n = qa.shape[0]
    rows = seq // GRID_W
    _, cls_of_block = _na_block_classes(rows)
    ncls = bias.shape[2]
    pairs = NA_HEADS // 2
    return pl.pallas_call(
        functools.partial(_na_kernel, rows=rows, cls_of_block=tuple(cls_of_block)),
        grid=(pairs, batch),
        in_specs=[
            pl.BlockSpec((seq, LANES), lambda p, b: (b, p)),
            pl.BlockSpec((None, seq // LANES, LANES, LANES), lambda p, b: (b, 0, p, 0)),
            pl.BlockSpec((seq, LANES), lambda p, b: (b, p)),
            pl.BlockSpec((None, 2, ncls, NA_Q, NA_K), lambda p, b: (e * pairs + p, 0, 0, 0, 0)),
        ],
        out_specs=pl.BlockSpec((seq, LANES), lambda p, b: (b, p)),
        out_shape=jax.ShapeDtypeStruct((n, D_A), BF16),
        compiler_params=_params("parallel", "parallel"),
        name="na_attention",
    )(qa, kat, va, bias)


def _log_sigmoid(x):
    return jnp.minimum(x, 0.0) - jnp.log1p(jnp.exp(-jnp.abs(x)))


def _mlstm_kernel(q_ref, kt_ref, v_ref, ob_ref, gt_ref, gn_ref, o_ref,
                  h_ref, src_ref, srcmax_ref, gtot_ref, bcols_ref, *, seq):
    L = ML_CHUNK
    nc = seq // L
    assert nc % 2 == 0
    cb = L // LANES
    log_scale = -0.5 * float(np.log(ML_HEAD_DIM))
    row_i = lax.broadcasted_iota(jnp.int32, (L, L), 0)
    col_i = lax.broadcasted_iota(jnp.int32, (L, L), 1)
    ones_col = (lax.broadcasted_iota(jnp.int32, (L, LANES), 1) == 0).astype(BF16)

    def tri_mask(reverse):
        return (col_i >= row_i) if reverse else (col_i <= row_i)

    def stat_row(c, head, reverse):
        return (head * nc + c) * SUBLANES + (2 if reverse else 0)

    def gate_prologue(head):
        hrows = slice(head * SUBLANES, (head + 1) * SUBLANES)
        x = jnp.concatenate(
            [jnp.concatenate([gt_ref[c * cb + j, hrows, :] for j in range(cb)], axis=1) for c in range(nc)],
            axis=0)
        n = nc * SUBLANES
        kind = lax.broadcasted_iota(jnp.int32, (n, 1), 0) % SUBLANES
        logf = _log_sigmoid(x)

        def split3(a):
            hi = a.astype(BF16)
            rest = a - hi.astype(F32)
            mid = rest.astype(BF16)
            return hi, mid, (rest - mid.astype(F32)).astype(BF16)

        pieces = split3(logf)
        pad = jnp.zeros((LANES - n, L), F32)
        pieces_t = split3(jnp.concatenate([logf, pad], axis=0).T)
        lower = tri_mask(False).astype(BF16)
        upper = tri_mask(True).astype(BF16)
        b_rows_f = sum(_dot(p, upper) for p in pieces)
        b_rows_r = sum(_dot(p, lower) for p in pieces)
        b_cols_f = sum(_dot(lower, p) for p in pieces_t)
        b_cols_r = sum(_dot(upper, p) for p in pieces_t)
        up1 = lambda a: pltpu.roll(a, n - 1, axis=0)
        src = x - jnp.where(kind == 0, up1(b_rows_f), up1(b_rows_r))
        rows = pl.ds(head * n, n)
        src_ref[rows, :] = src
        srcmax_ref[rows, :] = jnp.broadcast_to(jnp.max(src, axis=1, keepdims=True), (n, L))
        gtot_ref[rows, :] = up1(jnp.broadcast_to(jnp.sum(logf, axis=1, keepdims=True), (n, L)))
        bcols_ref[2 * head] = b_cols_f
        bcols_ref[2 * head + 1] = b_cols_r

    def stage_load(c, carry, head, reverse):
        rows = slice(c * L, (c + 1) * L)
        lanes = slice(head * ML_HEAD_DIM, (head + 1) * ML_HEAD_DIM)
        b_lane = c * SUBLANES + (3 if reverse else 1)
        st = dict(
            tri=tri_mask(reverse), row=stat_row(c, head, reverse),
            b_col=bcols_ref[2 * head + int(reverse), :, b_lane:b_lane + 1],
            q=q_ref[rows, lanes],
            kt=jnp.concatenate([kt_ref[c * cb + j, lanes, :] for j in range(cb)], axis=1),
            v_ext=jnp.concatenate([v_ref[rows, lanes], ones_col], axis=1),
            c_ext=carry[0], m=carry[1])
        st['qk'] = _dot(st['q'], st['kt'])
        st['qc'] = _dot(st['q'], st['c_ext'].astype(BF16))
        return st

    def stage_gates(st):
        m, slot = st['m'], pl.ds(st['row'], 1)
        src = src_ref[slot, :]
        mm = jnp.maximum(m, srcmax_ref[slot, :])
        st['w_src'] = jnp.exp(src + (log_scale - mm))
        st['w_carry'] = jnp.exp(m - mm)[:, :1]
        st['floor'] = jnp.exp(-(st['b_col'] + mm[:, :1]))
        st['m_new'] = gtot_ref[slot, :] + mm

    def stage_state(st):
        kv = _dot((st['kt'].astype(F32) * st['w_src']).astype(BF16), st['v_ext'])
        return st['w_carry'] * st['c_ext'] + kv, st['m_new']

    def stage_out(st):
        qkw = jnp.where(st['tri'], st['qk'] * st['w_src'], 0.0)
        tot = _dot(qkw.astype(BF16), st['v_ext']) + st['w_carry'] * st['qc']
        num = tot[:, :ML_HEAD_DIM]
        den = tot[:, ML_HEAD_DIM:ML_HEAD_DIM + 1]
        return num / jnp.maximum(jnp.abs(den), st['floor'])

    def finish(c, head, h):
        rows = slice(c * L, (c + 1) * L)
        lanes = slice(head * ML_HEAD_DIM, (head + 1) * ML_HEAD_DIM)
        mu = jnp.mean(h, axis=-1, keepdims=True)
        hc = h - mu
        var = jnp.mean(hc * hc, axis=-1, keepdims=True)
        hn = hc * lax.rsqrt(var + LN_EPS)
        gate = jax.nn.sigmoid(ob_ref[rows, lanes])
        o_ref[rows, lanes] = (gate * hn * gn_ref[head]).astype(BF16)

    def body(ci, carry, second_half):
        streams = []
        for head in range(ML_HEADS_PER_STEP):
            streams.append((ci, head, stage_load(ci, carry[2 * head], head, False)))
            streams.append((nc - 1 - ci, head, stage_load(nc - 1 - ci, carry[2 * head + 1], head, True)))
        for _, _, st in streams:
            stage_gates(st)
        new_carry = tuple(stage_state(st) for _, _, st in streams)
        for c, head, st in streams:
            h = stage_out(st)
            rows = slice(c * L, (c + 1) * L)
            lanes = slice(head * ML_HEAD_DIM, (head + 1) * ML_HEAD_DIM)
            if second_half:
                finish(c, head, h + h_ref[rows, lanes])
            else:
                h_ref[rows, lanes] = h
        return new_carry

    for head in range(ML_HEADS_PER_STEP):
        gate_prologue(head)

    zero = (jnp.zeros((ML_HEAD_DIM, 2 * LANES), F32), jnp.zeros((1, L), F32))
    carry = (zero,) * (2 * ML_HEADS_PER_STEP)
    for ci in range(nc):
        carry = body(ci, carry, second_half=ci >= nc // 2)


def _mlstm(qb, kbt, vb, ob, gt, gn_g, e, batch, seq):
    n = qb.shape[0]
    hps = ML_HEADS_PER_STEP
    width = hps * ML_HEAD_DIM
    tok = lambda b, g: (b, g)
    return pl.pallas_call(
        functools.partial(_mlstm_kernel, seq=seq),
        grid=(batch, ML_HEADS // hps),
        in_specs=[
            pl.BlockSpec((seq, width), tok),
            pl.BlockSpec((None, seq // LANES, width, LANES), lambda b, g: (b, 0, g, 0)),
            pl.BlockSpec((seq, width), tok),
            pl.BlockSpec((seq, width), tok),
            pl.BlockSpec((None, seq // LANES, hps * SUBLANES, LANES), lambda b, g: (b, 0, g, 0)),
            pl.BlockSpec((hps, 1, ML_HEAD_DIM), lambda b, g: (e * (ML_HEADS // hps) + g, 0, 0)),
        ],
        out_specs=pl.BlockSpec((seq, width), tok),
        out_shape=jax.ShapeDtypeStruct((n, D_B), BF16),
        scratch_shapes=[pltpu.VMEM((seq, width), F32)]
        + [pltpu.VMEM((hps * (seq // ML_CHUNK) * SUBLANES, ML_CHUNK), F32)] * 3
        + [pltpu.VMEM((2 * hps, ML_CHUNK, LANES), F32)],
        compiler_params=_params("parallel", "parallel"),
        name="mlstm",
    )(qb, kbt, vb, ob, gt, gn_g)


def _ab_out_kernel(x_ref, ya_ref, yb_ref, w_ref, g_ref, b_ref, o_ref):
    mix = _dot(ya_ref[...], w_ref[:D_A, :]) + _dot(yb_ref[...], w_ref[D_A:, :])
    o_ref[...] = _layer_norm(DEEPNORM_ALPHA * x_ref[...] + mix, g_ref[...], b_ref[...])


def _ab_out_ln(x, ya, yb, w_out, ln_g, ln_b, e, ln_idx):
    n = x.shape[0]
    tok = lambda i: (i, 0)
    return pl.pallas_call(
        _ab_out_kernel,
        grid=(n // PROJ_TM,),
        in_specs=[
            pl.BlockSpec((PROJ_TM, D_MODEL), tok),
            pl.BlockSpec((PROJ_TM, D_A), tok),
            pl.BlockSpec((PROJ_TM, D_B), tok),
            _resident((None, D_A + D_B, D_MODEL), lambda i: (e, 0, 0)),
            pl.BlockSpec((None, 1, D_MODEL), lambda i: (ln_idx, 0, 0)),
            pl.BlockSpec((None, 1, D_MODEL), lambda i: (ln_idx, 0, 0)),
        ],
        out_specs=pl.BlockSpec((PROJ_TM, D_MODEL), tok),
        out_shape=jax.ShapeDtypeStruct((n, D_MODEL), F32),
        compiler_params=_params("parallel"),
        name="ab_out_ln",
    )(x, ya, yb, w_out, ln_g, ln_b)


def _conv_kernel(x_ref, xp_ref, xn_ref, win_ref, cw_ref, cb_ref, wout_ref, g_ref, b_ref, o_ref,
                 *, tiles_per_seq):
    i = pl.program_id(0)
    D = D_MODEL
    x = x_ref[...]
    xb = x.astype(BF16)
    xe = jnp.concatenate([xp_ref[...], x, xn_ref[...]], axis=0).astype(BF16)
    ue = _dot(xe, win_ref[:, D:2 * D]) * _dot(xe, win_ref[:, 2 * D:])
    ext = PROJ_TM + 2 * SUBLANES
    tile = slice(SUBLANES, SUBLANES + PROJ_TM)
    u = ue[tile]
    u_m1 = pltpu.roll(ue, 1, axis=0)[tile]
    u_p1 = pltpu.roll(ue, ext - 1, axis=0)[tile]
    pos = i % tiles_per_seq
    row = lax.broadcasted_iota(jnp.int32, (PROJ_TM, 1), 0)
    u_m1 = jnp.where(jnp.logical_and(row == 0, pos == 0), 0.0, u_m1)
    u_p1 = jnp.where(jnp.logical_and(row == PROJ_TM - 1, pos == tiles_per_seq - 1), 0.0, u_p1)
    y = cw_ref[0:1, :] * u_m1 + cw_ref[1:2, :] * u + cw_ref[2:3, :] * u_p1 + cb_ref[...]
    z = (_dot(xb, win_ref[:, :D]) * y).astype(BF16)
    mix = _dot(z, wout_ref[...])
    o_ref[...] = _layer_norm(DEEPNORM_ALPHA * x + mix, g_ref[...], b_ref[...])


def _conv_mixer_ln(x, w_in, conv_w, conv_b, w_out, ln_g, ln_b, o, ln_idx, seq):
    n = x.shape[0]
    tiles_per_seq = seq // PROJ_TM
    rb = PROJ_TM // SUBLANES
    last_rb = n // SUBLANES - 1
    tok = lambda i: (i, 0)
    return pl.pallas_call(
        functools.partial(_conv_kernel, tiles_per_seq=tiles_per_seq),
        grid=(n // PROJ_TM,),
        in_specs=[
            pl.BlockSpec((PROJ_TM, D_MODEL), tok),
            pl.BlockSpec((SUBLANES, D_MODEL), lambda i: (jnp.maximum(i * rb - 1, 0), 0)),
            pl.BlockSpec((SUBLANES, D_MODEL), lambda i: (jnp.minimum((i + 1) * rb, last_rb), 0)),
            _resident((None, D_MODEL, 3 * D_MODEL), lambda i: (o, 0, 0)),
            pl.BlockSpec((None, 3, D_MODEL), lambda i: (o, 0, 0)),
            pl.BlockSpec((None, 1, D_MODEL), lambda i: (o, 0, 0)),
            _resident((None, D_MODEL, D_MODEL), lambda i: (o, 0, 0)),
            pl.BlockSpec((None, 1, D_MODEL), lambda i: (ln_idx, 0, 0)),
            pl.BlockSpec((None, 1, D_MODEL), lambda i: (ln_idx, 0, 0)),
        ],
        out_specs=pl.BlockSpec((PROJ_TM, D_MODEL), tok),
        out_shape=jax.ShapeDtypeStruct((n, D_MODEL), F32),
        compiler_params=_params("parallel"),
        name="conv_mixer_ln",
    )(x, x, x, w_in, conv_w, conv_b, w_out, ln_g, ln_b)


def _prep_ab_weights(ab_w_in, ab_gate_b):
    n_even = ab_w_in.shape[0]
    w = ab_w_in
    cut = lambda lo, hi: w[:, :, lo:hi]
    qa, ka, va = cut(0, D_A), cut(D_A, 2 * D_A), cut(2 * D_A, 3 * D_A)
    o0 = 3 * D_A
    qb, kb, vb, ob = (cut(o0, o0 + D_B), cut(o0 + D_B, o0 + 2 * D_B),
                      cut(o0 + 2 * D_B, o0 + 3 * D_B), cut(o0 + 3 * D_B, o0 + 4 * D_B))
    gates = cut(o0 + 4 * D_B, o0 + 4 * D_B + 4 * ML_HEADS)
    gates = gates.reshape(n_even, D_MODEL, 4, ML_HEADS).transpose(0, 3, 2, 1)
    gates = jnp.pad(gates, ((0, 0), (0, 0), (0, SUBLANES - 4), (0, 0))).reshape(n_even, _GATE_ROWS, D_MODEL)
    wnn = jnp.concatenate([qa, va, qb, vb, ob], axis=2).astype(BF16)
    wnt = jnp.concatenate([ka.transpose(0, 2, 1), kb.transpose(0, 2, 1), gates], axis=1).astype(BF16)
    gb = jnp.pad(ab_gate_b.astype(F32).transpose(0, 2, 1), ((0, 0), (0, 0), (0, SUBLANES - 4)))
    return wnn, wnt, gb.reshape(n_even, _GATE_ROWS, 1)


def kernel(x, ln_g, ln_b, ffn_w_in, ffn_w_out, ab_w_in, ab_gate_b, na_rpb, ml_gn_g, ab_w_out,
           sc_w_in, sc_conv_w, sc_conv_b, sc_w_out):
    batch, seq, d = x.shape
    rows = seq // GRID_W
    h = x.reshape(batch * seq, d)
    lng =ln_g.astype(F32).reshape(DEPTH * 3, 1, d)
    lnb = ln_b.astype(F32).reshape(DEPTH * 3, 1, d)
    wnn, wnt, gate_b = _prep_ab_weights(ab_w_in, ab_gate_b)
    ab_out = ab_w_out.astype(BF16)
    gn = ml_gn_g.astype(F32).reshape(-1, 1, ML_HEAD_DIM)
    sc_in = sc_w_in.astype(BF16)
    sc_out = sc_w_out.astype(BF16)
    conv_b = sc_conv_b.astype(F32).reshape(-1, 1, d)
    conv_w = sc_conv_w.astype(F32)
    na_bias = _na_bias_tables(na_rpb.astype(F32), rows)
    na_bias = na_bias.reshape(-1, 2, *na_bias.shape[2:])

    w_in, w_out = ffn_w_in[0, 0].astype(BF16), ffn_w_out[0, 0].astype(BF16)

    def ffn(h, w_in, w_out, layer, k):
        nxt = 2 * layer + k + 1
        if nxt == 2 * DEPTH:
            (h,) = _ffn_ln(h, w_in, w_out, lng, lnb, layer * 3 + 2 * k)
            return h, None, None
        return _ffn_ln(h, w_in, w_out, lng, lnb, layer * 3 + 2 * k, (ffn_w_in, ffn_w_out, nxt // 2, nxt % 2))

    for layer in range(DEPTH):
        h, w_in, w_out = ffn(h, w_in, w_out, layer, 0)
        if layer % 2 == 0:
            e = layer // 2
            qa, va, qb, vb, ob, kat, kbt, gt = _ab_proj(h, wnn, wnt, gate_b, e, batch, seq)
            ya = _na_attention(qa, kat, va, na_bias, e, batch, seq)
            yb = _mlstm(qb, kbt, vb, ob, gt, gn, e, batch, seq)
            h = _ab_out_ln(h, ya, yb, ab_out, lng, lnb, e, layer * 3 + 1)
        else:
            o = layer // 2
            h = _conv_mixer_ln(h, sc_in, conv_w, conv_b, sc_out, lng, lnb, o, layer * 3 + 1, seq)
        h, w_in, w_out = ffn(h, w_in, w_out, layer, 1)
    return h.reshape(batch, seq, d)
```

```python
import functools

import numpy as np
import jax
import jax.numpy as jnp
from jax import lax
from jax.experimental import pallas as pl
from jax.experimental.pallas import tpu as pltpu

D_MODEL = 1024
DEPTH = 4
GRID_W = 64
NA_HEAD_DIM = 64
NA_HEADS = 8
NA_WIN_H = 8
NA_WIN_W = 16
ML_HEADS = 4
ML_HEAD_DIM = 128
D_A = NA_HEADS * NA_HEAD_DIM
D_B = ML_HEADS * ML_HEAD_DIM
D_FF = 2816
DEEPNORM_ALPHA = (2 * DEPTH) ** 0.25
LN_EPS = 1e-5

LANES = 128
SUBLANES = 8
VMEM_LIMIT = 52 * 1024 * 1024

FFN_TM = 1024
FFN_ROWS = 512
FFN_TF = 256
FFN_CAST_OUT_BLOCKS = 16
PROJ_TM = 1024
ML_CHUNK = 256
ML_HEADS_PER_STEP = 2
NA_ROWS_PER_BLOCK = 2
NA_Q = NA_ROWS_PER_BLOCK * GRID_W
NA_KROWS = 10
NA_K = NA_KROWS * GRID_W
NA_MASKED = -1e30
NA_UNROLL = 1

BF16 = jnp.bfloat16
F32 = jnp.float32


def _dot(a, b):
    return jnp.dot(a, b, preferred_element_type=F32)


def _layer_norm(y, g, b):
    mu = jnp.mean(y, axis=-1, keepdims=True)
    yc = y - mu
    var = jnp.mean(yc * yc, axis=-1, keepdims=True)
    return yc * lax.rsqrt(var + LN_EPS) * g + b


def _params(*sem):
    return pltpu.CompilerParams(dimension_semantics=sem, vmem_limit_bytes=VMEM_LIMIT)


def _ffn_kernel(x_ref, wg_ref, wu_ref, wo_ref, g_ref, b_ref, *rest):
    if len(rest) == 1:
        (o_ref,) = rest
    else:
        nwi_ref, nwo_ref, o_ref, nwi_out_ref, nwo_out_ref = rest
        nwi_out_ref[...] = nwi_ref[...].astype(BF16)
        nwo_out_ref[...] = nwo_ref[...].astype(BF16)
    for r in range(FFN_TM // FFN_ROWS):
        rows = slice(r * FFN_ROWS, (r + 1) * FFN_ROWS)
        x = x_ref[rows, :]
        xb = x.astype(BF16)
        acc = None
        for c in range(D_FF // FFN_TF):
            cols = slice(c * FFN_TF, (c + 1) * FFN_TF)
            gate = _dot(xb, wg_ref[:, cols])
            up = _dot(xb, wu_ref[:, cols])
            act = (gate * jax.nn.sigmoid(gate) * up).astype(BF16)
            part = _dot(act, wo_ref[cols, :])
            acc = part if acc is None else acc + part
        o_ref[rows, :] = _layer_norm(DEEPNORM_ALPHA * x + 0.5 * acc, g_ref[...], b_ref[...])


def _resident(block_shape, index_map):
    return pl.BlockSpec(block_shape, index_map, pipeline_mode=pl.Buffered(1))


def _ffn_ln(x, w_in, w_out, ln_g, ln_b, ln_idx, next_f32=None):
    n = x.shape[0]
    steps = n // FFN_TM
    in_specs = [
        pl.BlockSpec((FFN_TM, D_MODEL), lambda i: (i, 0)),
        _resident((D_MODEL, D_FF), lambda i: (0, 0)),
        _resident((D_MODEL, D_FF), lambda i: (0, 1)),
        _resident((D_FF, D_MODEL), lambda i: (0, 0)),
        pl.BlockSpec((None, 1, D_MODEL), lambda i: (ln_idx, 0, 0)),
        pl.BlockSpec((None, 1, D_MODEL), lambda i: (ln_idx, 0, 0)),
    ]
    out_specs = [pl.BlockSpec((FFN_TM, D_MODEL), lambda i: (i, 0))]
    out_shape = [jax.ShapeDtypeStruct((n, D_MODEL), F32)]
    operands = [x, w_in, w_in, w_out, ln_g, ln_b]
    if next_f32 is not None:
        nwi, nwo, layer, k = next_f32
        wi_rows = D_MODEL // steps
        wo_rows = D_FF // FFN_CAST_OUT_BLOCKS
        visits = steps // FFN_CAST_OUT_BLOCKS
        in_specs += [pl.BlockSpec((None, None, wi_rows, 2 * D_FF), lambda i: (layer, k, i, 0)),
                     pl.BlockSpec((None, None, wo_rows, D_MODEL), lambda i: (layer, k, i // visits, 0))]
        out_specs += [pl.BlockSpec((wi_rows, 2 * D_FF), lambda i: (i, 0)),
                      pl.BlockSpec((wo_rows, D_MODEL), lambda i: (i // visits, 0))]
        out_shape += [jax.ShapeDtypeStruct((D_MODEL, 2 * D_FF), BF16),
                      jax.ShapeDtypeStruct((D_FF, D_MODEL), BF16)]
        operands += [nwi, nwo]
    return pl.pallas_call(
        _ffn_kernel,
        grid=(steps,),
        in_specs=in_specs,
        out_specs=out_specs,
        out_shape=out_shape,
        compiler_params=_params("arbitrary"),
        name="ffn_ln",
    )(*operands)


_NN_GROUPS = 5
_GATE_ROWS = ML_HEADS * SUBLANES


def _ab_proj_kernel(x_ref, wnn_ref, wnt_ref, gb_ref,
                    qa_ref, va_ref, qb_ref, vb_ref, ob_ref, kat_ref, kbt_ref, gt_ref):
    xb = x_ref[...].astype(BF16)

    def nn(g):
        return _dot(xb, wnn_ref[:, g * D_A:(g + 1) * D_A])

    def nt(lo, hi):
        return lax.dot_general(wnt_ref[lo:hi, :], xb, (((1,), (1,)), ((), ())),
                               preferred_element_type=F32)

    qa_ref[...] = (nn(0) * (NA_HEAD_DIM ** -0.5)).astype(BF16)
    va_ref[...] = nn(1).astype(BF16)
    qb_ref[...] = nn(2).astype(BF16)
    vb_ref[...] = nn(3).astype(BF16)
    ob_ref[...] = nn(4)
    kat = nt(0, D_A).astype(BF16)
    kbt = nt(D_A, D_A + D_B).astype(BF16)
    gt = nt(D_A + D_B, D_A + D_B + _GATE_ROWS) + gb_ref[...]
    for c in range(PROJ_TM // LANES):
        kat_ref[c] = kat[:, c * LANES:(c + 1) * LANES]
        kbt_ref[c] = kbt[:, c * LANES:(c + 1) * LANES]
        gt_ref[c] = gt[:, c * LANES:(c + 1) * LANES]


def _ab_proj(x, wnn, wnt, gate_b, e, batch, seq):
    n = x.shape[0]
    tiles_per_seq = seq // PROJ_TM
    cpt = PROJ_TM // LANES
    nblk = seq // LANES
    nt_rows = D_A + D_B + _GATE_ROWS
    tok = lambda i: (i, 0)
    tr = lambda i: (i // tiles_per_seq, i % tiles_per_seq, 0, 0)
    nat = lambda dt: jax.ShapeDtypeStruct((n, D_A), dt)
    return pl.pallas_call(
        _ab_proj_kernel,
        grid=(n // PROJ_TM,),
        in_specs=[
            pl.BlockSpec((PROJ_TM, D_MODEL), tok),
            _resident((None, D_MODEL, _NN_GROUPS * D_A), lambda i: (e, 0, 0)),
            _resident((None, nt_rows, D_MODEL), lambda i: (e, 0, 0)),
            pl.BlockSpec((None, _GATE_ROWS, 1), lambda i: (e, 0, 0)),
        ],
        out_specs=[
            pl.BlockSpec((PROJ_TM, D_A), tok),
            pl.BlockSpec((PROJ_TM, D_A), tok),
            pl.BlockSpec((PROJ_TM, D_A), tok),
            pl.BlockSpec((PROJ_TM, D_A), tok),
            pl.BlockSpec((PROJ_TM, D_A), tok),
            pl.BlockSpec((None, cpt, D_A, LANES), tr),
            pl.BlockSpec((None, cpt, D_B, LANES), tr),
            pl.BlockSpec((None, cpt, _GATE_ROWS, LANES), tr),
        ],
        out_shape=[
            nat(BF16), nat(BF16), nat(BF16), nat(BF16), nat(F32),
            jax.ShapeDtypeStruct((batch, nblk, D_A, LANES), BF16),
            jax.ShapeDtypeStruct((batch, nblk, D_B, LANES), BF16),
            jax.ShapeDtypeStruct((batch, nblk, _GATE_ROWS, LANES), F32),
        ],
        compiler_params=_params("parallel"),
        name="ab_proj",
    )(x, wnn, wnt, gate_b)


def _na_block_window_start(r0, rows):
    return np.clip(r0 - NA_WIN_H // 2, 0, rows - NA_KROWS)


def _na_block_classes(rows):
    reps = []
    cls_of_block = []
    seen = {}
    wh = min(NA_WIN_H, rows)
    for blk in range(rows // NA_ROWS_PER_BLOCK):
        r0 = blk * NA_ROWS_PER_BLOCK
        ws = _na_block_window_start(r0, rows)
        key = tuple(int(np.clip(r0 + d - wh // 2, 0, rows - wh)) - int(ws) for d in range(NA_ROWS_PER_BLOCK)) \
            + tuple(r0 + d - int(ws) for d in range(NA_ROWS_PER_BLOCK))
        if key not in seen:
            seen[key] = len(reps)
            reps.append(r0)
        cls_of_block.append(seen[key])
    return reps, cls_of_block


def _na_bias_tables(rpb, rows):
    reps, _ = _na_block_classes(rows)
    wh = min(NA_WIN_H, rows)
    n_dc = 2 * NA_WIN_W - 1
    col = np.arange(GRID_W)
    dc = np.clip(col[None, :] - col[:, None] + (NA_WIN_W - 1), 0, n_dc - 1)
    cs = np.clip(col - NA_WIN_W // 2, 0, GRID_W - NA_WIN_W)
    ok_c = (col[None, :] >= cs[:, None]) & (col[None, :] < cs[:, None] + NA_WIN_W)
    sel_dc = (dc[None] == np.arange(n_dc)[:, None, None]).astype(np.float32)
    by_col = jnp.einsum('...rd,dqk->...rqk', rpb, sel_dc, precision=lax.Precision.HIGHEST)
    by_col = jnp.where(ok_c, by_col, NA_MASKED).astype(F32)
    masked = jnp.full(by_col.shape[:-3] + (GRID_W, GRID_W), NA_MASKED, F32)
    tables = []
    for r0 in reps:
        ws = int(_na_block_window_start(r0, rows))
        q_rows = []
        for dq in range(NA_ROWS_PER_BLOCK):
            qr = r0 + dq
            rs = int(np.clip(qr - wh // 2, 0, rows - wh))
            slabs = [by_col[..., ws + i - qr + (NA_WIN_H - 1), :, :] if rs <= ws + i < rs + wh else masked
                     for i in range(NA_KROWS)]
            q_rows.append(jnp.concatenate(slabs, axis=-1))
        tables.append(jnp.concatenate(q_rows, axis=-2))
    return jnp.stack(tables, axis=-3)


def _na_kernel(q_ref, kt_ref, v_ref, bias_ref, o_ref, *, rows, cls_of_block):
    lane = lax.broadcasted_iota(jnp.int32, (NA_Q, LANES), 1)
    first_head = lane < NA_HEAD_DIM
    kblocks = NA_K // LANES

    def window_start(blk):
        return int(_na_block_window_start(blk * NA_ROWS_PER_BLOCK, rows))

    def score_stage(group):
        scores = []
        for blk in group:
            q2 = q_ref[blk * NA_Q:(blk + 1) * NA_Q, :]
            kb0 = window_start(blk) * GRID_W // LANES
            kwin = jnp.concatenate([kt_ref[kb0 + c] for c in range(kblocks)], axis=1)
            for h in range(2):
                qh = jnp.where(first_head if h == 0 else jnp.logical_not(first_head), q2, jnp.zeros_like(q2))
                scores.append(_dot(qh, kwin) + bias_ref[h, cls_of_block[blk]])
        return scores

    def output_stage(group, scores):
        probs = []
        for s in scores:
            p = jnp.exp(s - jnp.max(s, axis=-1, keepdims=True))
            probs.append((p.astype(BF16), jnp.sum(p, axis=-1, keepdims=True)))
        for u, blk in enumerate(group):
            k0 = window_start(blk) * GRID_W
            vwin = v_ref[k0:k0 + NA_K, :]
            outs = [_dot(p, vwin) / l for p, l in probs[2 * u:2 * u + 2]]
            o_ref[blk * NA_Q:(blk + 1) * NA_Q, :] = jnp.where(first_head, outs[0], outs[1]).astype(BF16)

    nblk = rows // NA_ROWS_PER_BLOCK
    groups = [range(g, g + NA_UNROLL) for g in range(0, nblk, NA_UNROLL)]
    scores = score_stage(groups[0])
    for g, group in enumerate(groups):
        ahead = score_stage(groups[g + 1]) if g + 1 < len(groups) else None
        output_stage(group, scores)
        scores = ahead


def _na_attention(qa, kat, va, bias, e, batch, seq):
    n = qa.shape[0]
    rows = seq // GRID_W
    _, cls_of_block = _na_block_classes(rows)
    ncls = bias.shape[2]
    pairs = NA_HEADS // 2
    return pl.pallas_call(
        functools.partial(_na_kernel, rows=rows, cls_of_block=tuple(cls_of_block)),
        grid=(pairs, batch),
        in_specs=[
            pl.BlockSpec((seq, LANES), lambda p, b: (b, p)),
            pl.BlockSpec((None, seq // LANES, LANES, LANES), lambda p, b: (b, 0, p, 0)),
            pl.BlockSpec((seq, LANES), lambda p, b: (b, p)),
            pl.BlockSpec((None, 2, ncls, NA_Q, NA_K), lambda p, b: (e * pairs + p, 0, 0, 0, 0)),
        ],
        out_specs=pl.BlockSpec((seq, LANES), lambda p, b: (b, p)),
        out_shape=jax.ShapeDtypeStruct((n, D_A), BF16),
        compiler_params=_params("parallel", "parallel"),
        name="na_attention",
    )(qa, kat, va, bias)


def _log_sigmoid(x):
    return jnp.minimum(x, 0.0) - jnp.log1p(jnp.exp(-jnp.abs(x)))


def _mlstm_kernel(q_ref, kt_ref, v_ref, ob_ref, gt_ref, gn_ref, o_ref,
                  h_ref, src_ref, srcmax_ref, gtot_ref, bcols_ref, *, seq):
    L = ML_CHUNK
    nc = seq // L
    assert nc % 2 == 0
    cb = L // LANES
    log_scale = -0.5 * float(np.log(ML_HEAD_DIM))
    row_i = lax.broadcasted_iota(jnp.int32, (L, L), 0)
    col_i = lax.broadcasted_iota(jnp.int32, (L, L), 1)
    ones_col = (lax.broadcasted_iota(jnp.int32, (L, LANES), 1) == 0).astype(BF16)

    def tri_mask(reverse):
        return (col_i >= row_i) if reverse else (col_i <= row_i)

    def stat_row(c, head, reverse):
        return (head * nc + c) * SUBLANES + (2 if reverse else 0)

    def gate_prologue(head):
        hrows = slice(head * SUBLANES, (head + 1) * SUBLANES)
        x = jnp.concatenate(
            [jnp.concatenate([gt_ref[c * cb + j, hrows, :] for j in range(cb)], axis=1) for c in range(nc)],
            axis=0)
        n = nc * SUBLANES
        kind = lax.broadcasted_iota(jnp.int32, (n, 1), 0) % SUBLANES
        logf = _log_sigmoid(x)

        def split3(a):
            hi = a.astype(BF16)
            rest = a - hi.astype(F32)
            mid = rest.astype(BF16)
            return hi, mid, (rest - mid.astype(F32)).astype(BF16)

        pieces = split3(logf)
        pad = jnp.zeros((LANES - n, L), F32)
        pieces_t = split3(jnp.concatenate([logf, pad], axis=0).T)
        lower = tri_mask(False).astype(BF16)
        upper = tri_mask(True).astype(BF16)
        b_rows_f = sum(_dot(p, upper) for p in pieces)
        b_rows_r = sum(_dot(p, lower) for p in pieces)
        b_cols_f = sum(_dot(lower, p) for p in pieces_t)
        b_cols_r = sum(_dot(upper, p) for p in pieces_t)
        up1 = lambda a: pltpu.roll(a, n - 1, axis=0)
        src = x - jnp.where(kind == 0, up1(b_rows_f), up1(b_rows_r))
        rows = pl.ds(head * n, n)
        src_ref[rows, :] = src
        srcmax_ref[rows, :] = jnp.broadcast_to(jnp.max(src, axis=1, keepdims=True), (n, L))
        gtot_ref[rows, :] = up1(jnp.broadcast_to(jnp.sum(logf, axis=1, keepdims=True), (n, L)))
        bcols_ref[2 * head] = b_cols_f
        bcols_ref[2 * head + 1] = b_cols_r

    def stage_load(c, carry, head, reverse):
        rows = slice(c * L, (c + 1) * L)
        lanes = slice(head * ML_HEAD_DIM, (head + 1) * ML_HEAD_DIM)
        b_lane = c * SUBLANES + (3 if reverse else 1)
        st = dict(
            tri=tri_mask(reverse), row=stat_row(c, head, reverse),
            b_col=bcols_ref[2 * head + int(reverse), :, b_lane:b_lane + 1],
            q=q_ref[rows, lanes],
            kt=jnp.concatenate([kt_ref[c * cb + j, lanes, :] for j in range(cb)], axis=1),
            v_ext=jnp.concatenate([v_ref[rows, lanes], ones_col], axis=1),
            c_ext=carry[0], m=carry[1])
        st['qk'] = _dot(st['q'], st['kt'])
        st['qc'] = _dot(st['q'], st['c_ext'].astype(BF16))
        return st

    def stage_gates(st):
        m, slot = st['m'], pl.ds(st['row'], 1)
        src = src_ref[slot, :]
        mm = jnp.maximum(m, srcmax_ref[slot, :])
        st['w_src'] = jnp.exp(src + (log_scale - mm))
        st['w_carry'] = jnp.exp(m - mm)[:, :1]
        st['floor'] = jnp.exp(-(st['b_col'] + mm[:, :1]))
        st['m_new'] = gtot_ref[slot, :] + mm

    def stage_state(st):
        kv = _dot((st['kt'].astype(F32) * st['w_src']).astype(BF16), st['v_ext'])
        return st['w_carry'] * st['c_ext'] + kv, st['m_new']

    def stage_out(st):
        qkw = jnp.where(st['tri'], st['qk'] * st['w_src'], 0.0)
        tot = _dot(qkw.astype(BF16), st['v_ext']) + st['w_carry'] * st['qc']
        num = tot[:, :ML_HEAD_DIM]
        den = tot[:, ML_HEAD_DIM:ML_HEAD_DIM + 1]
        return num / jnp.maximum(jnp.abs(den), st['floor'])

    def finish(c, head, h):
        rows = slice(c * L, (c + 1) * L)
        lanes = slice(head * ML_HEAD_DIM, (head + 1) * ML_HEAD_DIM)
        mu = jnp.mean(h, axis=-1, keepdims=True)
        hc = h - mu
        var = jnp.mean(hc * hc, axis=-1, keepdims=True)
        hn = hc * lax.rsqrt(var + LN_EPS)
        gate = jax.nn.sigmoid(ob_ref[rows, lanes])
        o_ref[rows, lanes] = (gate * hn * gn_ref[head]).astype(BF16)

    def body(ci, carry, second_half):
        streams = []
        for head in range(ML_HEADS_PER_STEP):
            streams.append((ci, head, stage_load(ci, carry[2 * head], head, False)))
            streams.append((nc - 1 - ci, head, stage_load(nc - 1 - ci, carry[2 * head + 1], head, True)))
        for _, _, st in streams:
            stage_gates(st)
        new_carry = tuple(stage_state(st) for _, _, st in streams)
        for c, head, st in streams:
            h = stage_out(st)
            rows = slice(c * L, (c + 1) * L)
            lanes = slice(head * ML_HEAD_DIM, (head + 1) * ML_HEAD_DIM)
            if second_half:
                finish(c, head, h + h_ref[rows, lanes])
            else:
                h_ref[rows, lanes] = h
        return new_carry

    for head in range(ML_HEADS_PER_STEP):
        gate_prologue(head)

    zero = (jnp.zeros((ML_HEAD_DIM, 2 * LANES), F32), jnp.zeros((1, L), F32))
    carry = (zero,) * (2 * ML_HEADS_PER_STEP)
    for ci in range(nc):
        carry = body(ci, carry, second_half=ci >= nc // 2)


def _mlstm(qb, kbt, vb, ob, gt, gn_g, e, batch, seq):
    n = qb.shape[0]
    hps = ML_HEADS_PER_STEP
    width = hps * ML_HEAD_DIM
    tok = lambda b, g: (b, g)
    return pl.pallas_call(
        functools.partial(_mlstm_kernel, seq=seq),
        grid=(batch, ML_HEADS // hps),
        in_specs=[
            pl.BlockSpec((seq, width), tok),
            pl.BlockSpec((None, seq // LANES, width, LANES), lambda b, g: (b, 0, g, 0)),
            pl.BlockSpec((seq, width), tok),
            pl.BlockSpec((seq, width), tok),
            pl.BlockSpec((None, seq // LANES, hps * SUBLANES, LANES), lambda b, g: (b, 0, g, 0)),
            pl.BlockSpec((hps, 1, ML_HEAD_DIM), lambda b, g: (e * (ML_HEADS // hps) + g, 0, 0)),
        ],
        out_specs=pl.BlockSpec((seq, width), tok),
        out_shape=jax.ShapeDtypeStruct((n, D_B), BF16),
        scratch_shapes=[pltpu.VMEM((seq, width), F32)]
        + [pltpu.VMEM((hps * (seq // ML_CHUNK) * SUBLANES, ML_CHUNK), F32)] * 3
        + [pltpu.VMEM((2 * hps, ML_CHUNK, LANES), F32)],
        compiler_params=_params("parallel", "parallel"),
        name="mlstm",
    )(qb, kbt, vb, ob, gt, gn_g)


def _ab_out_kernel(x_ref, ya_ref, yb_ref, w_ref, g_ref, b_ref, o_ref):
    mix = _dot(ya_ref[...], w_ref[:D_A, :]) + _dot(yb_ref[...], w_ref[D_A:, :])
    o_ref[...] = _layer_norm(DEEPNORM_ALPHA * x_ref[...] + mix, g_ref[...], b_ref[...])


def _ab_out_ln(x, ya, yb, w_out, ln_g, ln_b, e, ln_idx):
    n = x.shape[0]
    tok = lambda i: (i, 0)
    return pl.pallas_call(
        _ab_out_kernel,
        grid=(n // PROJ_TM,),
        in_specs=[
            pl.BlockSpec((PROJ_TM, D_MODEL), tok),
            pl.BlockSpec((PROJ_TM, D_A), tok),
            pl.BlockSpec((PROJ_TM, D_B), tok),
            _resident((None, D_A + D_B, D_MODEL), lambda i: (e, 0, 0)),
            pl.BlockSpec((None, 1, D_MODEL), lambda i: (ln_idx, 0, 0)),
            pl.BlockSpec((None, 1, D_MODEL), lambda i: (ln_idx, 0, 0)),
        ],
        out_specs=pl.BlockSpec((PROJ_TM, D_MODEL), tok),
        out_shape=jax.ShapeDtypeStruct((n, D_MODEL), F32),
        compiler_params=_params("parallel"),
        name="ab_out_ln",
    )(x, ya, yb, w_out, ln_g, ln_b)


def _conv_kernel(x_ref, xp_ref, xn_ref, win_ref, cw_ref, cb_ref, wout_ref, g_ref, b_ref, o_ref,
                 *, tiles_per_seq):
    i = pl.program_id(0)
    D = D_MODEL
    x = x_ref[...]
    xb = x.astype(BF16)
    xe = jnp.concatenate([xp_ref[...], x, xn_ref[...]], axis=0).astype(BF16)
    ue = _dot(xe, win_ref[:, D:2 * D]) * _dot(xe, win_ref[:, 2 * D:])
    ext = PROJ_TM + 2 * SUBLANES
    tile = slice(SUBLANES, SUBLANES + PROJ_TM)
    u = ue[tile]
    u_m1 = pltpu.roll(ue, 1, axis=0)[tile]
    u_p1 = pltpu.roll(ue, ext - 1, axis=0)[tile]
    pos = i % tiles_per_seq
    row = lax.broadcasted_iota(jnp.int32, (PROJ_TM, 1), 0)
    u_m1 = jnp.where(jnp.logical_and(row == 0, pos == 0), 0.0, u_m1)
    u_p1 = jnp.where(jnp.logical_and(row == PROJ_TM - 1, pos == tiles_per_seq - 1), 0.0, u_p1)
    y = cw_ref[0:1, :] * u_m1 + cw_ref[1:2, :] * u + cw_ref[2:3, :] * u_p1 + cb_ref[...]
    z = (_dot(xb, win_ref[:, :D]) * y).astype(BF16)
    mix = _dot(z, wout_ref[...])
    o_ref[...] = _layer_norm(DEEPNORM_ALPHA * x + mix, g_ref[...], b_ref[...])


def _conv_mixer_ln(x, w_in, conv_w, conv_b, w_out, ln_g, ln_b, o, ln_idx, seq):
    n = x.shape[0]
    tiles_per_seq = seq // PROJ_TM
    rb = PROJ_TM // SUBLANES
    last_rb = n // SUBLANES - 1
    tok = lambda i: (i, 0)
    return pl.pallas_call(
        functools.partial(_conv_kernel, tiles_per_seq=tiles_per_seq),
        grid=(n // PROJ_TM,),
        in_specs=[
            pl.BlockSpec((PROJ_TM, D_MODEL), tok),
            pl.BlockSpec((SUBLANES, D_MODEL), lambda i: (jnp.maximum(i * rb - 1, 0), 0)),
            pl.BlockSpec((SUBLANES, D_MODEL), lambda i: (jnp.minimum((i + 1) * rb, last_rb), 0)),
            _resident((None, D_MODEL, 3 * D_MODEL), lambda i: (o, 0, 0)),
            pl.BlockSpec((None, 3, D_MODEL), lambda i: (o, 0, 0)),
            pl.BlockSpec((None, 1, D_MODEL), lambda i: (o, 0, 0)),
            _resident((None, D_MODEL, D_MODEL), lambda i: (o, 0, 0)),
            pl.BlockSpec((None, 1, D_MODEL), lambda i: (ln_idx, 0, 0)),
            pl.BlockSpec((None, 1, D_MODEL), lambda i: (ln_idx, 0, 0)),
        ],
        out_specs=pl.BlockSpec((PROJ_TM, D_MODEL), tok),
        out_shape=jax.ShapeDtypeStruct((n, D_MODEL), F32),
        compiler_params=_params("parallel"),
        name="conv_mixer_ln",
    )(x, x, x, w_in, conv_w, conv_b, w_out, ln_g, ln_b)


def _prep_ab_weights(ab_w_in, ab_gate_b):
    n_even = ab_w_in.shape[0]
    w = ab_w_in
    cut = lambda lo, hi: w[:, :, lo:hi]
    qa, ka, va = cut(0, D_A), cut(D_A, 2 * D_A), cut(2 * D_A, 3 * D_A)
    o0 = 3 * D_A
    qb, kb, vb, ob = (cut(o0, o0 + D_B), cut(o0 + D_B, o0 + 2 * D_B),
                      cut(o0 + 2 * D_B, o0 + 3 * D_B), cut(o0 + 3 * D_B, o0 + 4 * D_B))
    gates = cut(o0 + 4 * D_B, o0 + 4 * D_B + 4 * ML_HEADS)
    gates = gates.reshape(n_even, D_MODEL, 4, ML_HEADS).transpose(0, 3, 2, 1)
    gates = jnp.pad(gates, ((0, 0), (0, 0), (0, SUBLANES - 4), (0, 0))).reshape(n_even, _GATE_ROWS, D_MODEL)
    wnn = jnp.concatenate([qa, va, qb, vb, ob], axis=2).astype(BF16)
    wnt = jnp.concatenate([ka.transpose(0, 2, 1), kb.transpose(0, 2, 1), gates], axis=1).astype(BF16)
    gb = jnp.pad(ab_gate_b.astype(F32).transpose(0, 2, 1), ((0, 0), (0, 0), (0, SUBLANES - 4)))
    return wnn, wnt, gb.reshape(n_even, _GATE_ROWS, 1)


def kernel(x, ln_g, ln_b, ffn_w_in, ffn_w_out, ab_w_in, ab_gate_b, na_rpb, ml_gn_g, ab_w_out,
           sc_w_in, sc_conv_w, sc_conv_b, sc_w_out):
    batch, seq, d = x.shape
    rows = seq // GRID_W
    h = x.reshape(batch * seq, d)
    lng =ln_g.astype(F32).reshape(DEPTH * 3, 1, d)
    lnb = ln_b.astype(F32).reshape(DEPTH * 3, 1, d)
    wnn, wnt, gate_b = _prep_ab_weights(ab_w_in, ab_gate_b)
    ab_out = ab_w_out.astype(BF16)
    gn = ml_gn_g.astype(F32).reshape(-1, 1, ML_HEAD_DIM)
    sc_in = sc_w_in.astype(BF16)
    sc_out = sc_w_out.astype(BF16)
    conv_b = sc_conv_b.astype(F32).reshape(-1, 1, d)
    conv_w = sc_conv_w.astype(F32)
    na_bias = _na_bias_tables(na_rpb.astype(F32), rows)
    na_bias = na_bias.reshape(-1, 2, *na_bias.shape[2:])

    w_in, w_out = ffn_w_in[0, 0].astype(BF16), ffn_w_out[0, 0].astype(BF16)

    def ffn(h, w_in, w_out, layer, k):
        nxt = 2 * layer + k + 1
        if nxt == 2 * DEPTH:
            (h,) = _ffn_ln(h, w_in, w_out, lng, lnb, layer * 3 + 2 * k)
            return h, None, None
        return _ffn_ln(h, w_in, w_out, lng, lnb, layer * 3 + 2 * k, (ffn_w_in, ffn_w_out, nxt // 2, nxt % 2))

    for layer in range(DEPTH):
        h, w_in, w_out = ffn(h, w_in, w_out, layer, 0)
        if layer % 2 == 0:
            e = layer // 2
            qa, va, qb, vb, ob, kat, kbt, gt = _ab_proj(h, wnn, wnt, gate_b, e, batch, seq)
            ya = _na_attention(qa, kat, va, na_bias, e, batch, seq)
            yb = _mlstm(qb, kbt, vb, ob, gt, gn, e, batch, seq)
            h = _ab_out_ln(h, ya, yb, ab_out, lng, lnb, e, layer * 3 + 1)
        else:
            o = layer // 2
            h = _conv_mixer_ln(h, sc_in, conv_w, conv_b, sc_out, lng, lnb, o, layer * 3 + 1, seq)
        h, w_in, w_out = ffn(h, w_in, w_out, layer, 1)
    return h.reshape(batch, seq, d)
```

```python
import functools

import numpy as np
import jax
import jax.numpy as jnp
from jax import lax
from jax.experimental import pallas as pl
from jax.experimental.pallas import tpu as pltpu

D_MODEL = 1024
DEPTH = 4
GRID_W = 64
NA_HEAD_DIM = 64
NA_HEADS = 8
NA_WIN_H = 8
NA_WIN_W = 16
ML_HEADS = 4
ML_HEAD_DIM = 128
D_A = NA_HEADS * NA_HEAD_DIM
D_B = ML_HEADS * ML_HEAD_DIM
D_FF = 2816
DEEPNORM_ALPHA = (2 * DEPTH) ** 0.25
LN_EPS = 1e-5

LANES = 128
SUBLANES = 8
VMEM_LIMIT = 52 * 1024 * 1024

FFN_TM = 1024
FFN_ROWS = 512
FFN_TF = 256
FFN_CAST_OUT_BLOCKS = 16
PROJ_TM = 1024
ML_CHUNK = 256
ML_HEADS_PER_STEP = 2
NA_ROWS_PER_BLOCK = 2
NA_Q = NA_ROWS_PER_BLOCK * GRID_W
NA_KROWS = 10
NA_K = NA_KROWS * GRID_W
NA_MASKED = -1e30
NA_UNROLL = 1

BF16 = jnp.bfloat16
F32 = jnp.float32


def _dot(a, b):
    return jnp.dot(a, b, preferred_element_type=F32)


def _layer_norm(y, g, b):
    mu = jnp.mean(y, axis=-1, keepdims=True)
    yc = y - mu
    var = jnp.mean(yc * yc, axis=-1, keepdims=True)
    return yc * lax.rsqrt(var + LN_EPS) * g + b


def _params(*sem):
    return pltpu.CompilerParams(dimension_semantics=sem, vmem_limit_bytes=VMEM_LIMIT)


def _ffn_kernel(*refs, mix_in, cast_next):
    refs = list(refs)
    x_ref = refs.pop(0)
    if mix_in:
        ya_ref, yb_ref, wm_ref, gm_ref, bm_ref = refs[:5]
        del refs[:5]
    wg_ref, wu_ref, wo_ref, g_ref, b_ref = refs[:5]
    del refs[:5]
    if cast_next:
        nwi_ref, nwo_ref, o_ref, nwi_out_ref, nwo_out_ref = refs
        nwi_out_ref[...] = nwi_ref[...].astype(BF16)
        nwo_out_ref[...] = nwo_ref[...].astype(BF16)
    else:
        (o_ref,) = refs
    for r in range(FFN_TM // FFN_ROWS):
        rows = slice(r * FFN_ROWS, (r + 1) * FFN_ROWS)
        x = x_ref[rows, :]
        if mix_in:
            mix = _dot(ya_ref[rows, :], wm_ref[:D_A, :]) + _dot(yb_ref[rows, :], wm_ref[D_A:, :])
            x = _layer_norm(DEEPNORM_ALPHA * x + mix, gm_ref[...], bm_ref[...])
        xb = x.astype(BF16)
        acc = None
        for c in range(D_FF // FFN_TF):
            cols = slice(c * FFN_TF, (c + 1) * FFN_TF)
            gate = _dot(xb, wg_ref[:, cols])
            up = _dot(xb, wu_ref[:, cols])
            act = (gate * jax.nn.sigmoid(gate) * up).astype(BF16)
            part = _dot(act, wo_ref[cols, :])
            acc = part if acc is None else acc + part
        o_ref[rows, :] = _layer_norm(DEEPNORM_ALPHA * x + 0.5 * acc, g_ref[...], b_ref[...])


def _resident(block_shape, index_map):
    return pl.BlockSpec(block_shape, index_map, pipeline_mode=pl.Buffered(1))


def _ffn_ln(x, w_in, w_out, ln_g, ln_b, ln_idx, next_f32=None, mix=None):
    n = x.shape[0]
    steps = n // FFN_TM
    tok = lambda i: (i, 0)
    ln_spec = lambda idx: pl.BlockSpec((None, 1, D_MODEL), lambda i: (idx, 0, 0))
    in_specs = [pl.BlockSpec((FFN_TM, D_MODEL), tok)]
    operands = [x]
    if mix is not None:
        ya, yb, w_mix, e, mix_ln_idx = mix
        in_specs += [pl.BlockSpec((FFN_TM, D_A), tok), pl.BlockSpec((FFN_TM, D_B), tok),
                     _resident((None, D_A + D_B, D_MODEL), lambda i: (e, 0, 0)),
                     ln_spec(mix_ln_idx), ln_spec(mix_ln_idx)]
        operands += [ya, yb, w_mix, ln_g, ln_b]
    in_specs += [
        _resident((D_MODEL, D_FF), lambda i: (0, 0)),
        _resident((D_MODEL, D_FF), lambda i: (0, 1)),
        _resident((D_FF, D_MODEL), lambda i: (0, 0)),
        ln_spec(ln_idx), ln_spec(ln_idx),
    ]
    operands += [w_in, w_in, w_out, ln_g, ln_b]
    out_specs = [pl.BlockSpec((FFN_TM, D_MODEL), tok)]
    out_shape = [jax.ShapeDtypeStruct((n, D_MODEL), F32)]
    if next_f32 is not None:
        nwi, nwo, layer, k = next_f32
        wi_rows = D_MODEL // steps
        wo_rows = D_FF // FFN_CAST_OUT_BLOCKS
        visits = steps // FFN_CAST_OUT_BLOCKS
        in_specs += [pl.BlockSpec((None, None, wi_rows, 2 * D_FF), lambda i: (layer, k, i, 0)),
                     pl.BlockSpec((None, None, wo_rows, D_MODEL), lambda i: (layer, k, i // visits, 0))]
        out_specs += [pl.BlockSpec((wi_rows, 2 * D_FF), lambda i: (i, 0)),
                      pl.BlockSpec((wo_rows, D_MODEL), lambda i: (i // visits, 0))]
        out_shape += [jax.ShapeDtypeStruct((D_MODEL, 2 * D_FF), BF16),
                      jax.ShapeDtypeStruct((D_FF, D_MODEL), BF16)]
        operands += [nwi, nwo]
    return pl.pallas_call(
        functools.partial(_ffn_kernel, mix_in=mix is not None, cast_next=next_f32 is not None),
        grid=(steps,),
        in_specs=in_specs,
        out_specs=out_specs,
        out_shape=out_shape,
        compiler_params=_params("arbitrary"),
        name="ffn_ln",
    )(*operands)


_NN_GROUPS = 5
_GATE_ROWS = ML_HEADS * SUBLANES


def _ab_proj_kernel(x_ref, wnn_ref, wnt_ref, gb_ref,
                    qa_ref, va_ref, qb_ref, vb_ref, ob_ref, kat_ref, kbt_ref, gt_ref):
    xb = x_ref[...].astype(BF16)

    def nn(g):
        return _dot(xb, wnn_ref[:, g * D_A:(g + 1) * D_A])

    def nt(lo, hi):
        return lax.dot_general(wnt_ref[lo:hi, :], xb, (((1,), (1,)), ((), ())),
                               preferred_element_type=F32)

    qa_ref[...] = (nn(0) * (NA_HEAD_DIM ** -0.5)).astype(BF16)
    va_ref[...] = nn(1).astype(BF16)
    qb_ref[...] = nn(2).astype(BF16)
    vb_ref[...] = nn(3).astype(BF16)
    ob_ref[...] = nn(4)
    kat = nt(0, D_A).astype(BF16)
    kbt = nt(D_A, D_A + D_B).astype(BF16)
    gt = nt(D_A + D_B, D_A + D_B + _GATE_ROWS) + gb_ref[...]
    for c in range(PROJ_TM // LANES):
        kat_ref[c] = kat[:, c * LANES:(c + 1) * LANES]
        kbt_ref[c] = kbt[:, c * LANES:(c + 1) * LANES]
        gt_ref[c] = gt[:, c * LANES:(c + 1) * LANES]


def _ab_proj(x, wnn, wnt, gate_b, e, batch, seq):
    n = x.shape[0]
    tiles_per_seq = seq // PROJ_TM
    cpt = PROJ_TM // LANES
    nblk = seq // LANES
    nt_rows = D_A + D_B + _GATE_ROWS
    tok = lambda i: (i, 0)
    tr = lambda i: (i // tiles_per_seq, i % tiles_per_seq, 0, 0)
    nat = lambda dt: jax.ShapeDtypeStruct((n, D_A), dt)
    return pl.pallas_call(
        _ab_proj_kernel,
        grid=(n // PROJ_TM,),
        in_specs=[
            pl.BlockSpec((PROJ_TM, D_MODEL), tok),
            _resident((None, D_MODEL, _NN_GROUPS * D_A), lambda i: (e, 0, 0)),
            _resident((None, nt_rows, D_MODEL), lambda i: (e, 0, 0)),
            pl.BlockSpec((None, _GATE_ROWS, 1), lambda i: (e, 0, 0)),
        ],
        out_specs=[
            pl.BlockSpec((PROJ_TM, D_A), tok),
            pl.BlockSpec((PROJ_TM, D_A), tok),
            pl.BlockSpec((PROJ_TM, D_A), tok),
            pl.BlockSpec((PROJ_TM, D_A), tok),
            pl.BlockSpec((PROJ_TM, D_A), tok),
            pl.BlockSpec((None, cpt, D_A, LANES), tr),
            pl.BlockSpec((None, cpt, D_B, LANES), tr),
            pl.BlockSpec((None, cpt, _GATE_ROWS, LANES), tr),
        ],
        out_shape=[
            nat(BF16), nat(BF16), nat(BF16), nat(BF16), nat(F32),
            jax.ShapeDtypeStruct((batch, nblk, D_A, LANES), BF16),
            jax.ShapeDtypeStruct((batch, nblk, D_B, LANES), BF16),
            jax.ShapeDtypeStruct((batch, nblk, _GATE_ROWS, LANES), F32),
        ],
        compiler_params=_params("parallel"),
        name="ab_proj",
    )(x, wnn, wnt, gate_b)


def _na_block_window_start(r0, rows):
    return np.clip(r0 - NA_WIN_H // 2, 0, rows - NA_KROWS)


def _na_block_classes(rows):
    reps = []
    cls_of_block = []
    seen = {}
    wh = min(NA_WIN_H, rows)
    for blk in range(rows // NA_ROWS_PER_BLOCK):
        r0 = blk * NA_ROWS_PER_BLOCK
        ws = _na_block_window_start(r0, rows)
        key = tuple(int(np.clip(r0 + d - wh // 2, 0, rows - wh)) - int(ws) for d in range(NA_ROWS_PER_BLOCK)) \
            + tuple(r0 + d - int(ws) for d in range(NA_ROWS_PER_BLOCK))
        if key not in seen:
            seen[key] = len(reps)
            reps.append(r0)
        cls_of_block.append(seen[key])
    return reps, cls_of_block


def _na_bias_tables(rpb, rows):
    reps, _ = _na_block_classes(rows)
    wh = min(NA_WIN_H, rows)
    n_dc = 2 * NA_WIN_W - 1
    col = np.arange(GRID_W)
    dc = np.clip(col[None, :] - col[:, None] + (NA_WIN_W - 1), 0, n_dc - 1)
    cs = np.clip(col - NA_WIN_W // 2, 0, GRID_W - NA_WIN_W)
    ok_c = (col[None, :] >= cs[:, None]) & (col[None, :] < cs[:, None] + NA_WIN_W)
    sel_dc = (dc[None] == np.arange(n_dc)[:, None, None]).astype(np.float32)
    by_col = jnp.einsum('...rd,dqk->...rqk', rpb, sel_dc, precision=lax.Precision.HIGHEST)
    by_col = jnp.where(ok_c, by_col, NA_MASKED).astype(F32)
    masked = jnp.full(by_col.shape[:-3] + (GRID_W, GRID_W), NA_MASKED, F32)
    tables = []
    for r0 in reps:
        ws = int(_na_block_window_start(r0, rows))
        q_rows = []
        for dq in range(NA_ROWS_PER_BLOCK):
            qr = r0 + dq
            rs = int(np.clip(qr - wh // 2, 0, rows - wh))
            slabs = [by_col[..., ws + i - qr + (NA_WIN_H - 1), :, :] if rs <= ws + i < rs + wh else masked
                     for i in range(NA_KROWS)]
            q_rows.append(jnp.concatenate(slabs, axis=-1))
        tables.append(jnp.concatenate(q_rows, axis=-2))
    return jnp.stack(tables, axis=-3)


def _na_kernel(q_ref, kt_ref, v_ref, bias_ref, o_ref, *, rows, cls_of_block):
    lane = lax.broadcasted_iota(jnp.int32, (NA_Q, LANES), 1)
    first_head = lane < NA_HEAD_DIM
    kblocks = NA_K // LANES

    def window_start(blk):
        return int(_na_block_window_start(blk * NA_ROWS_PER_BLOCK, rows))

    def score_stage(group):
        scores = []
        for blk in group:
            q2 = q_ref[blk * NA_Q:(blk + 1) * NA_Q, :]
            kb0 = window_start(blk) * GRID_W // LANES
            kwin = jnp.concatenate([kt_ref[kb0 + c] for c in range(kblocks)], axis=1)
            for h in range(2):
                qh = jnp.where(first_head if h == 0 else jnp.logical_not(first_head), q2, jnp.zeros_like(q2))
                scores.append(_dot(qh, kwin) + bias_ref[h, cls_of_block[blk]])
        return scores

    def output_stage(group, scores):
        probs = []
        for s in scores:
            p = jnp.exp(s - jnp.max(s, axis=-1, keepdims=True))
            probs.append((p.astype(BF16), jnp.sum(p, axis=-1, keepdims=True)))
        for u, blk in enumerate(group):
            k0 = window_start(blk) * GRID_W
            vwin = v_ref[k0:k0 + NA_K, :]
            outs = [_dot(p, vwin) / l for p, l in probs[2 * u:2 * u + 2]]
            o_ref[blk * NA_Q:(blk + 1) * NA_Q, :] = jnp.where(first_head, outs[0], outs[1]).astype(BF16)

    nblk = rows // NA_ROWS_PER_BLOCK
    groups = [range(g, g + NA_UNROLL) for g in range(0, nblk, NA_UNROLL)]
    scores = score_stage(groups[0])
    for g, group in enumerate(groups):
        ahead = score_stage(groups[g + 1]) if g + 1 < len(groups) else None
        output_stage(group, scores)
        scores = ahead


def _na_attention(qa, kat, va, bias, e, batch, seq):
    n = qa.shape[0]
    rows = seq // GRID_W
    _, cls_of_block = _na_block_classes(rows)
    ncls = bias.shape[2]
    pairs = NA_HEADS // 2
    return pl.pallas_call(
        functools.partial(_na_kernel, rows=rows, cls_of_block=tuple(cls_of_block)),
        grid=(pairs, batch),
        in_specs=[
            pl.BlockSpec((seq, LANES), lambda p, b: (b, p)),
            pl.BlockSpec((None, seq // LANES, LANES, LANES), lambda p, b: (b, 0, p, 0)),
            pl.BlockSpec((seq, LANES), lambda p, b: (b, p)),
            pl.BlockSpec((None, 2, ncls, NA_Q, NA_K), lambda p, b: (e * pairs + p, 0, 0, 0, 0)),
        ],
        out_specs=pl.BlockSpec((seq, LANES), lambda p, b: (b, p)),
        out_shape=jax.ShapeDtypeStruct((n, D_A), BF16),
        compiler_params=_params("parallel", "parallel"),
        name="na_attention",
    )(qa, kat, va, bias)


def _log_sigmoid(x):
    return jnp.minimum(x, 0.0) - jnp.log1p(jnp.exp(-jnp.abs(x)))


def _mlstm_kernel(q_ref, kt_ref, v_ref, ob_ref, gt_ref, gn_ref, o_ref,
                  h_ref, src_ref, srcmax_ref, gtot_ref, bcols_ref, *, seq):
    L = ML_CHUNK
    nc = seq // L
    assert nc % 2 == 0
    cb = L // LANES
    log_scale = -0.5 * float(np.log(ML_HEAD_DIM))
    row_i = lax.broadcasted_iota(jnp.int32, (L, L), 0)
    col_i = lax.broadcasted_iota(jnp.int32, (L, L), 1)
    ones_col = (lax.broadcasted_iota(jnp.int32, (L, LANES), 1) == 0).astype(BF16)

    def tri_mask(reverse):
        return (col_i >= row_i) if reverse else (col_i <= row_i)

    def stat_row(c, head, reverse):
        return (head * nc + c) * SUBLANES + (2 if reverse else 0)

    def gate_prologue(head):
        hrows = slice(head * SUBLANES, (head + 1) * SUBLANES)
        x = jnp.concatenate(
            [jnp.concatenate([gt_ref[c * cb + j, hrows, :] for j in range(cb)], axis=1) for c in range(nc)],
            axis=0)
        n = nc * SUBLANES
        kind = lax.broadcasted_iota(jnp.int32, (n, 1), 0) % SUBLANES
        logf = _log_sigmoid(x)

        def split3(a):
            hi = a.astype(BF16)
            rest = a - hi.astype(F32)
            mid = rest.astype(BF16)
            return hi, mid, (rest - mid.astype(F32)).astype(BF16)

        pieces = split3(logf)
        pad = jnp.zeros((LANES - n, L), F32)
        pieces_t = split3(jnp.concatenate([logf, pad], axis=0).T)
        lower = tri_mask(False).astype(BF16)
        upper = tri_mask(True).astype(BF16)
        b_rows_f = sum(_dot(p, upper) for p in pieces)
        b_rows_r = sum(_dot(p, lower) for p in pieces)
        b_cols_f = sum(_dot(lower, p) for p in pieces_t)
        b_cols_r = sum(_dot(upper, p) for p in pieces_t)
        up1 = lambda a: pltpu.roll(a, n - 1, axis=0)
        src = x - jnp.where(kind == 0, up1(b_rows_f), up1(b_rows_r))
        rows = pl.ds(head * n, n)
        src_ref[rows, :] = src
        srcmax_ref[rows, :] = jnp.broadcast_to(jnp.max(src, axis=1, keepdims=True), (n, L))
        gtot_ref[rows, :] = up1(jnp.broadcast_to(jnp.sum(logf, axis=1, keepdims=True), (n, L)))
        bcols_ref[2 * head] = b_cols_f
        bcols_ref[2 * head + 1] = b_cols_r

    def stage_load(c, carry, head, reverse):
        rows = slice(c * L, (c + 1) * L)
        lanes = slice(head * ML_HEAD_DIM, (head + 1) * ML_HEAD_DIM)
        b_lane = c * SUBLANES + (3 if reverse else 1)
        st = dict(
            tri=tri_mask(reverse), row=stat_row(c, head, reverse),
            b_col=bcols_ref[2 * head + int(reverse), :, b_lane:b_lane + 1],
            q=q_ref[rows, lanes],
            kt=jnp.concatenate([kt_ref[c * cb + j, lanes, :] for j in range(cb)], axis=1),
            v_ext=jnp.concatenate([v_ref[rows, lanes], ones_col], axis=1),
            c_ext=carry[0], m=carry[1])
        st['qk'] = _dot(st['q'], st['kt'])
        st['qc'] = _dot(st['q'], st['c_ext'].astype(BF16))
        return st

    def stage_gates(st):
        m, slot = st['m'], pl.ds(st['row'], 1)
        src = src_ref[slot, :]
        mm = jnp.maximum(m, srcmax_ref[slot, :])
        st['w_src'] = jnp.exp(src + (log_scale - mm))
        st['w_carry'] = jnp.exp(m - mm)[:, :1]
        st['floor'] = jnp.exp(-(st['b_col'] + mm[:, :1]))
        st['m_new'] = gtot_ref[slot, :] + mm

    def stage_state(st):
        kv = _dot((st['kt'].astype(F32) * st['w_src']).astype(BF16), st['v_ext'])
        return st['w_carry'] * st['c_ext'] + kv, st['m_new']

    def stage_out(st):
        qkw = jnp.where(st['tri'], st['qk'] * st['w_src'], 0.0)
        tot = _dot(qkw.astype(BF16), st['v_ext']) + st['w_carry'] * st['qc']
        num = tot[:, :ML_HEAD_DIM]
        den = tot[:, ML_HEAD_DIM:ML_HEAD_DIM + 1]
        return num / jnp.maximum(jnp.abs(den), st['floor'])

    def finish(c, head, h):
        rows = slice(c * L, (c + 1) * L)
        lanes = slice(head * ML_HEAD_DIM, (head + 1) * ML_HEAD_DIM)
        mu = jnp.mean(h, axis=-1, keepdims=True)
        hc = h - mu
        var = jnp.mean(hc * hc, axis=-1, keepdims=True)
        hn = hc * lax.rsqrt(var + LN_EPS)
        gate = jax.nn.sigmoid(ob_ref[rows, lanes])
        o_ref[rows, lanes] = (gate * hn * gn_ref[head]).astype(BF16)

    def body(ci, carry, second_half):
        streams = []
        for head in range(ML_HEADS_PER_STEP):
            streams.append((ci, head, stage_load(ci, carry[2 * head], head, False)))
            streams.append((nc - 1 - ci, head, stage_load(nc - 1 - ci, carry[2 * head + 1], head, True)))
        for _, _, st in streams:
            stage_gates(st)
        new_carry = tuple(stage_state(st) for _, _, st in streams)
        for c, head, st in streams:
            h = stage_out(st)
            rows = slice(c * L, (c + 1) * L)
            lanes = slice(head * ML_HEAD_DIM, (head + 1) * ML_HEAD_DIM)
            if second_half:
                finish(c, head, h + h_ref[rows, lanes])
            else:
                h_ref[rows, lanes] = h
        return new_carry

    for head in range(ML_HEADS_PER_STEP):
        gate_prologue(head)

    zero = (jnp.zeros((ML_HEAD_DIM, 2 * LANES), F32), jnp.zeros((1, L), F32))
    carry = (zero,) * (2 * ML_HEADS_PER_STEP)
    for ci in range(nc):
        carry = body(ci, carry, second_half=ci >= nc // 2)


def _mlstm(qb, kbt, vb, ob, gt, gn_g, e, batch, seq):
    n = qb.shape[0]
    hps = ML_HEADS_PER_STEP
    width = hps * ML_HEAD_DIM
    tok = lambda b, g: (b, g)
    return pl.pallas_call(
        functools.partial(_mlstm_kernel, seq=seq),
        grid=(batch, ML_HEADS // hps),
        in_specs=[
            pl.BlockSpec((seq, width), tok),
            pl.BlockSpec((None, seq // LANES, width, LANES), lambda b, g: (b, 0, g, 0)),
            pl.BlockSpec((seq, width), tok),
            pl.BlockSpec((seq, width), tok),
            pl.BlockSpec((None, seq // LANES, hps * SUBLANES, LANES), lambda b, g: (b, 0, g, 0)),
            pl.BlockSpec((hps, 1, ML_HEAD_DIM), lambda b, g: (e * (ML_HEADS // hps) + g, 0, 0)),
        ],
        out_specs=pl.BlockSpec((seq, width), tok),
        out_shape=jax.ShapeDtypeStruct((n, D_B), BF16),
        scratch_shapes=[pltpu.VMEM((seq, width), F32)]
        + [pltpu.VMEM((hps * (seq // ML_CHUNK) * SUBLANES, ML_CHUNK), F32)] * 3
        + [pltpu.VMEM((2 * hps, ML_CHUNK, LANES), F32)],
        compiler_params=_params("parallel", "parallel"),
        name="mlstm",
    )(qb, kbt, vb, ob, gt, gn_g)


def _conv_kernel(x_ref, xp_ref, xn_ref, win_ref, cw_ref, cb_ref, wout_ref, g_ref, b_ref, o_ref,
                 *, tiles_per_seq):
    i = pl.program_id(0)
    D = D_MODEL
    x = x_ref[...]
    xb = x.astype(BF16)
    xe = jnp.concatenate([xp_ref[...], x, xn_ref[...]], axis=0).astype(BF16)
    ue = _dot(xe, win_ref[:, D:2 * D]) * _dot(xe, win_ref[:, 2 * D:])
    ext = PROJ_TM + 2 * SUBLANES
    tile = slice(SUBLANES, SUBLANES + PROJ_TM)
    u = ue[tile]
    u_m1 = pltpu.roll(ue, 1, axis=0)[tile]
    u_p1 = pltpu.roll(ue, ext - 1, axis=0)[tile]
    pos = i % tiles_per_seq
    row = lax.broadcasted_iota(jnp.int32, (PROJ_TM, 1), 0)
    u_m1 = jnp.where(jnp.logical_and(row == 0, pos == 0), 0.0, u_m1)
    u_p1 = jnp.where(jnp.logical_and(row == PROJ_TM - 1, pos == tiles_per_seq - 1), 0.0, u_p1)
    y = cw_ref[0:1, :] * u_m1 + cw_ref[1:2, :] * u + cw_ref[2:3, :] * u_p1 + cb_ref[...]
    z = (_dot(xb, win_ref[:, :D]) * y).astype(BF16)
    mix = _dot(z, wout_ref[...])
    o_ref[...] = _layer_norm(DEEPNORM_ALPHA * x + mix, g_ref[...], b_ref[...])


def _conv_mixer_ln(x, w_in, conv_w, conv_b, w_out, ln_g, ln_b, o, ln_idx, seq):
    n = x.shape[0]
    tiles_per_seq = seq // PROJ_TM
    rb = PROJ_TM // SUBLANES
    last_rb = n // SUBLANES - 1
    tok = lambda i: (i, 0)
    return pl.pallas_call(
        functools.partial(_conv_kernel, tiles_per_seq=tiles_per_seq),
        grid=(n // PROJ_TM,),
        in_specs=[
            pl.BlockSpec((PROJ_TM, D_MODEL), tok),
            pl.BlockSpec((SUBLANES, D_MODEL), lambda i: (jnp.maximum(i * rb - 1, 0), 0)),
            pl.BlockSpec((SUBLANES, D_MODEL), lambda i: (jnp.minimum((i + 1) * rb, last_rb), 0)),
            _resident((None, D_MODEL, 3 * D_MODEL), lambda i: (o, 0, 0)),
            pl.BlockSpec((None, 3, D_MODEL), lambda i: (o, 0, 0)),
            pl.BlockSpec((None, 1, D_MODEL), lambda i: (o, 0, 0)),
            _resident((None, D_MODEL, D_MODEL), lambda i: (o, 0, 0)),
            pl.BlockSpec((None, 1, D_MODEL), lambda i: (ln_idx, 0, 0)),
            pl.BlockSpec((None, 1, D_MODEL), lambda i: (ln_idx, 0, 0)),
        ],
        out_specs=pl.BlockSpec((PROJ_TM, D_MODEL), tok),
        out_shape=jax.ShapeDtypeStruct((n, D_MODEL), F32),
        compiler_params=_params("parallel"),
        name="conv_mixer_ln",
    )(x, x, x, w_in, conv_w, conv_b, w_out, ln_g, ln_b)


def _prep_ab_weights(ab_w_in, ab_gate_b):
    n_even = ab_w_in.shape[0]
    w = ab_w_in
    cut = lambda lo, hi: w[:, :, lo:hi]
    qa, ka, va = cut(0, D_A), cut(D_A, 2 * D_A), cut(2 * D_A, 3 * D_A)
    o0 = 3 * D_A
    qb, kb, vb, ob = (cut(o0, o0 + D_B), cut(o0 + D_B, o0 + 2 * D_B),
                      cut(o0 + 2 * D_B, o0 + 3 * D_B), cut(o0 + 3 * D_B, o0 + 4 * D_B))
    gates = cut(o0 + 4 * D_B, o0 + 4 * D_B + 4 * ML_HEADS)
    gates = gates.reshape(n_even, D_MODEL, 4, ML_HEADS).transpose(0, 3, 2, 1)
    gates = jnp.pad(gates, ((0, 0), (0, 0), (0, SUBLANES - 4), (0, 0))).reshape(n_even, _GATE_ROWS, D_MODEL)
    wnn = jnp.concatenate([qa, va, qb, vb, ob], axis=2).astype(BF16)
    wnt = jnp.concatenate([ka.transpose(0, 2, 1), kb.transpose(0, 2, 1), gates], axis=1).astype(BF16)
    gb = jnp.pad(ab_gate_b.astype(F32).transpose(0, 2, 1), ((0, 0), (0, 0), (0, SUBLANES - 4)))
    return wnn, wnt, gb.reshape(n_even, _GATE_ROWS, 1)


def kernel(x, ln_g, ln_b, ffn_w_in, ffn_w_out, ab_w_in, ab_gate_b, na_rpb, ml_gn_g, ab_w_out,
           sc_w_in, sc_conv_w, sc_conv_b, sc_w_out):
    batch, seq, d = x.shape
    rows = seq // GRID_W
    h = x.reshape(batch * seq, d)
    lng =ln_g.astype(F32).reshape(DEPTH * 3, 1, d)
    lnb = ln_b.astype(F32).reshape(DEPTH * 3, 1, d)
    wnn, wnt, gate_b = _prep_ab_weights(ab_w_in, ab_gate_b)
    ab_out = ab_w_out.astype(BF16)
    gn = ml_gn_g.astype(F32).reshape(-1, 1, ML_HEAD_DIM)
    sc_in = sc_w_in.astype(BF16)
    sc_out = sc_w_out.astype(BF16)
    conv_b = sc_conv_b.astype(F32).reshape(-1, 1, d)
    conv_w = sc_conv_w.astype(F32)
    na_bias = _na_bias_tables(na_rpb.astype(F32), rows)
    na_bias = na_bias.reshape(-1, 2, *na_bias.shape[2:])

    w_in, w_out = ffn_w_in[0, 0].astype(BF16), ffn_w_out[0, 0].astype(BF16)

    def ffn(h, w_in, w_out, layer, k, mix=None):
        nxt = 2 * layer + k + 1
        if nxt == 2 * DEPTH:
            (h,) = _ffn_ln(h, w_in, w_out, lng, lnb, layer * 3 + 2 * k, mix=mix)
            return h, None, None
        return _ffn_ln(h, w_in, w_out, lng, lnb, layer * 3 + 2 * k,
                       (ffn_w_in, ffn_w_out, nxt // 2, nxt % 2), mix)

    for layer in range(DEPTH):
        h, w_in, w_out = ffn(h, w_in, w_out, layer, 0)
        if layer % 2 == 0:
            e = layer // 2
            qa, va, qb, vb, ob, kat, kbt, gt = _ab_proj(h, wnn, wnt, gate_b, e, batch, seq)
            ya = _na_attention(qa, kat, va, na_bias, e, batch, seq)
            yb = _mlstm(qb, kbt, vb, ob, gt, gn, e, batch, seq)
            mix = (ya, yb, ab_out, e, layer * 3 + 1)
        else:
            o = layer // 2
            h = _conv_mixer_ln(h, sc_in, conv_w, conv_b, sc_out, lng, lnb, o, layer * 3 + 1, seq)
            mix = None
        h, w_in, w_out = ffn(h, w_in, w_out, layer, 1, mix)
    return h.reshape(batch, seq, d)
```

```python
import functools

import numpy as np
import jax
import jax.numpy as jnp
from jax import lax
from jax.experimental import pallas as pl
from jax.experimental.pallas import tpu as pltpu

D_MODEL = 1024
DEPTH = 4
GRID_W = 64
NA_HEAD_DIM = 64
NA_HEADS = 8
NA_WIN_H = 8
NA_WIN_W = 16
ML_HEADS = 4
ML_HEAD_DIM = 128
D_A = NA_HEADS * NA_HEAD_DIM
D_B = ML_HEADS * ML_HEAD_DIM
D_FF = 2816
DEEPNORM_ALPHA = (2 * DEPTH) ** 0.25
LN_EPS = 1e-5

LANES = 128
SUBLANES = 8
VMEM_LIMIT = 52 * 1024 * 1024

FFN_TM = 1024
FFN_ROWS = 512
FFN_TF = 256
FFN_CAST_OUT_BLOCKS = 16
PROJ_TM = 1024
ML_CHUNK = 256
ML_HEADS_PER_STEP = 2
NA_ROWS_PER_BLOCK = 2
NA_Q = NA_ROWS_PER_BLOCK * GRID_W
NA_KROWS = 10
NA_K = NA_KROWS * GRID_W
NA_MASKED = -1e30
NA_UNROLL = 1

BF16 = jnp.bfloat16
F32 = jnp.float32


def _dot(a, b):
    return jnp.dot(a, b, preferred_element_type=F32)


def _layer_norm(y, g, b):
    mu = jnp.mean(y, axis=-1, keepdims=True)
    yc = y - mu
    var = jnp.mean(yc * yc, axis=-1, keepdims=True)
    return yc * lax.rsqrt(var + LN_EPS) * g + b


def _params(*sem):
    return pltpu.CompilerParams(dimension_semantics=sem, vmem_limit_bytes=VMEM_LIMIT)


def _ffn_kernel(*refs, mix_in, cast_next):
    refs = list(refs)
    x_ref = refs.pop(0)
    if mix_in:
        ya_ref, yb_ref, wm_ref, gm_ref, bm_ref = refs[:5]
        del refs[:5]
    wg_ref, wu_ref, wo_ref, g_ref, b_ref = refs[:5]
    del refs[:5]
    if cast_next:
        nwi_ref, nwo_ref, o_ref, nwi_out_ref, nwo_out_ref = refs
        nwi_out_ref[...] = nwi_ref[...].astype(BF16)
        nwo_out_ref[...] = nwo_ref[...].astype(BF16)
    else:
        (o_ref,) = refs
    for r in range(FFN_TM // FFN_ROWS):
        rows = slice(r * FFN_ROWS, (r + 1) * FFN_ROWS)
        x = x_ref[rows, :]
        if mix_in:
            mix = _dot(ya_ref[rows, :], wm_ref[:D_A, :]) + _dot(yb_ref[rows, :], wm_ref[D_A:, :])
            x = _layer_norm(DEEPNORM_ALPHA * x + mix, gm_ref[...], bm_ref[...])
        xb = x.astype(BF16)
        acc = None
        for c in range(D_FF // FFN_TF):
            cols = slice(c * FFN_TF, (c + 1) * FFN_TF)
            gate = _dot(xb, wg_ref[:, cols])
            up = _dot(xb, wu_ref[:, cols])
            act = (gate * jax.nn.sigmoid(gate) * up).astype(BF16)
            part = _dot(act, wo_ref[cols, :])
            acc = part if acc is None else acc + part
        o_ref[rows, :] = _layer_norm(DEEPNORM_ALPHA * x + 0.5 * acc, g_ref[...], b_ref[...])


def _resident(block_shape, index_map):
    return pl.BlockSpec(block_shape, index_map, pipeline_mode=pl.Buffered(1))


def _ffn_ln(x, w_in, w_out, ln_g, ln_b, ln_idx, next_f32=None, mix=None):
    n = x.shape[0]
    steps = n // FFN_TM
    tok = lambda i: (i, 0)
    ln_spec = lambda idx: pl.BlockSpec((None, 1, D_MODEL), lambda i: (idx, 0, 0))
    in_specs = [pl.BlockSpec((FFN_TM, D_MODEL), tok)]
    operands = [x]
    if mix is not None:
        ya, yb, w_mix, e, mix_ln_idx = mix
        in_specs += [pl.BlockSpec((FFN_TM, D_A), tok), pl.BlockSpec((FFN_TM, D_B), tok),
                     _resident((None, D_A + D_B, D_MODEL), lambda i: (e, 0, 0)),
                     ln_spec(mix_ln_idx), ln_spec(mix_ln_idx)]
        operands += [ya, yb, w_mix, ln_g, ln_b]
    in_specs += [
        _resident((D_MODEL, D_FF), lambda i: (0, 0)),
        _resident((D_MODEL, D_FF), lambda i: (0, 1)),
        _resident((D_FF, D_MODEL), lambda i: (0, 0)),
        ln_spec(ln_idx), ln_spec(ln_idx),
    ]
    operands += [w_in, w_in, w_out, ln_g, ln_b]
    out_specs = [pl.BlockSpec((FFN_TM, D_MODEL), tok)]
    out_shape = [jax.ShapeDtypeStruct((n, D_MODEL), F32)]
    if next_f32 is not None:
        nwi, nwo, layer, k = next_f32
        wi_rows = D_MODEL // steps
        wo_rows = D_FF // FFN_CAST_OUT_BLOCKS
        visits = steps // FFN_CAST_OUT_BLOCKS
        in_specs += [pl.BlockSpec((None, None, wi_rows, 2 * D_FF), lambda i: (layer, k, i, 0)),
                     pl.BlockSpec((None, None, wo_rows, D_MODEL), lambda i: (layer, k, i // visits, 0))]
        out_specs += [pl.BlockSpec((wi_rows, 2 * D_FF), lambda i: (i, 0)),
                      pl.BlockSpec((wo_rows, D_MODEL), lambda i: (i // visits, 0))]
        out_shape += [jax.ShapeDtypeStruct((D_MODEL, 2 * D_FF), BF16),
                      jax.ShapeDtypeStruct((D_FF, D_MODEL), BF16)]
        operands += [nwi, nwo]
    return pl.pallas_call(
        functools.partial(_ffn_kernel, mix_in=mix is not None, cast_next=next_f32 is not None),
        grid=(steps,),
        in_specs=in_specs,
        out_specs=out_specs,
        out_shape=out_shape,
        compiler_params=_params("arbitrary"),
        name="ffn_ln",
    )(*operands)


_NN_GROUPS = 5
_GATE_ROWS = ML_HEADS * SUBLANES


def _ab_proj_kernel(x_ref, wnn_ref, wnt_ref, gb_ref,
                    qa_ref, va_ref, qb_ref, vb_ref, ob_ref, kat_ref, kbt_ref, gt_ref):
    xb = x_ref[...].astype(BF16)

    def nn(g):
        return _dot(xb, wnn_ref[:, g * D_A:(g + 1) * D_A])

    def nt(lo, hi):
        return lax.dot_general(wnt_ref[lo:hi, :], xb, (((1,), (1,)), ((), ())),
                               preferred_element_type=F32)

    qa_ref[...] = (nn(0) * (NA_HEAD_DIM ** -0.5)).astype(BF16)
    va_ref[...] = nn(1).astype(BF16)
    qb_ref[...] = nn(2).astype(BF16)
    vb_ref[...] = nn(3).astype(BF16)
    ob_ref[...] = nn(4)
    kat = nt(0, D_A).astype(BF16)
    kbt = nt(D_A, D_A + D_B).astype(BF16)
    gt = nt(D_A + D_B, D_A + D_B + _GATE_ROWS) + gb_ref[...]
    for c in range(PROJ_TM // LANES):
        kat_ref[c] = kat[:, c * LANES:(c + 1) * LANES]
        kbt_ref[c] = kbt[:, c * LANES:(c + 1) * LANES]
        gt_ref[c] = gt[:, c * LANES:(c + 1) * LANES]


def _ab_proj(x, wnn, wnt, gate_b, e, batch, seq):
    n = x.shape[0]
    tiles_per_seq = seq // PROJ_TM
    cpt = PROJ_TM // LANES
    nblk = seq // LANES
    nt_rows = D_A + D_B + _GATE_ROWS
    tok = lambda i: (i, 0)
    tr = lambda i: (i // tiles_per_seq, i % tiles_per_seq, 0, 0)
    nat = lambda dt: jax.ShapeDtypeStruct((n, D_A), dt)
    return pl.pallas_call(
        _ab_proj_kernel,
        grid=(n // PROJ_TM,),
        in_specs=[
            pl.BlockSpec((PROJ_TM, D_MODEL), tok),
            _resident((None, D_MODEL, _NN_GROUPS * D_A), lambda i: (e, 0, 0)),
            _resident((None, nt_rows, D_MODEL), lambda i: (e, 0, 0)),
            pl.BlockSpec((None, _GATE_ROWS, 1), lambda i: (e, 0, 0)),
        ],
        out_specs=[
            pl.BlockSpec((PROJ_TM, D_A), tok),
            pl.BlockSpec((PROJ_TM, D_A), tok),
            pl.BlockSpec((PROJ_TM, D_A), tok),
            pl.BlockSpec((PROJ_TM, D_A), tok),
            pl.BlockSpec((PROJ_TM, D_A), tok),
            pl.BlockSpec((None, cpt, D_A, LANES), tr),
            pl.BlockSpec((None, cpt, D_B, LANES), tr),
            pl.BlockSpec((None, cpt, _GATE_ROWS, LANES), tr),
        ],
        out_shape=[
            nat(BF16), nat(BF16), nat(BF16), nat(BF16), nat(F32),
            jax.ShapeDtypeStruct((batch, nblk, D_A, LANES), BF16),
            jax.ShapeDtypeStruct((batch, nblk, D_B, LANES), BF16),
            jax.ShapeDtypeStruct((batch, nblk, _GATE_ROWS, LANES), F32),
        ],
        compiler_params=_params("parallel"),
        name="ab_proj",
    )(x, wnn, wnt, gate_b)


def _na_block_window_start(r0, rows):
    return np.clip(r0 - NA_WIN_H // 2, 0, rows - NA_KROWS)


_NA_SLABS = 2 * NA_WIN_H
_NA_MASKED_SLAB = _NA_SLABS - 1


def _na_slab(rows, blk, dq, i):
    wh = min(NA_WIN_H, rows)
    qr = blk * NA_ROWS_PER_BLOCK + dq
    kr = int(_na_block_window_start(blk * NA_ROWS_PER_BLOCK, rows)) + i
    rs = int(np.clip(qr - wh // 2, 0, rows - wh))
    return kr - qr + (NA_WIN_H - 1) if rs <= kr < rs + wh else _NA_MASKED_SLAB


def _na_tile_index(rows):
    pairs, where = [], {}
    for blk in range(rows // NA_ROWS_PER_BLOCK):
        for dq in range(NA_ROWS_PER_BLOCK):
            for j in range(NA_K // LANES):
                pair = (_na_slab(rows, blk, dq, 2 * j), _na_slab(rows, blk, dq, 2 * j + 1))
                if pair not in pairs:
                    pairs.append(pair)
                where[blk, dq, j] = pairs.index(pair)
    return pairs, where


def _na_bias_tables(rpb, rows):
    hi = lax.Precision.HIGHEST
    n_dc = 2 * NA_WIN_W - 1
    col = np.arange(GRID_W)
    dc = np.clip(col[None, :] - col[:, None] + (NA_WIN_W - 1), 0, n_dc - 1)
    cs = np.clip(col - NA_WIN_W // 2, 0, GRID_W - NA_WIN_W)
    ok_c = (col[None, :] >= cs[:, None]) & (col[None, :] < cs[:, None] + NA_WIN_W)
    sel_dc = (dc[None] == np.arange(n_dc)[:, None, None]).astype(np.float32)
    by_col = jnp.einsum('...rd,dqk->...rqk', rpb, sel_dc, precision=hi)
    by_col = jnp.where(ok_c, by_col, NA_MASKED).astype(F32)
    masked = jnp.full(by_col.shape[:-3] + (1, GRID_W, GRID_W), NA_MASKED, F32)
    slabs = jnp.concatenate([by_col, masked], axis=-3)
    pairs, _ = _na_tile_index(rows)
    pick = lambda side: (np.array([p[side] for p in pairs])[:, None] == np.arange(_NA_SLABS)).astype(np.float32)
    halves = [jnp.einsum('us,...sqk->...uqk', pick(side), slabs, precision=hi) for side in (0, 1)]
    return jnp.concatenate(halves, axis=-1)


def _na_kernel(q_ref, kt_ref, v_ref, bias_ref, o_ref, *, rows):
    lane = lax.broadcasted_iota(jnp.int32, (NA_Q, LANES), 1)
    first_head = lane < NA_HEAD_DIM
    kblocks = NA_K // LANES
    _, tile_of = _na_tile_index(rows)

    def bias(h, blk):
        return jnp.concatenate(
            [jnp.concatenate([bias_ref[h, tile_of[blk, dq, j]] for j in range(kblocks)], axis=1)
             for dq in range(NA_ROWS_PER_BLOCK)], axis=0)

    def window_start(blk):
        return int(_na_block_window_start(blk * NA_ROWS_PER_BLOCK, rows))

    def score_stage(group):
        scores = []
        for blk in group:
            q2 = q_ref[blk * NA_Q:(blk + 1) * NA_Q, :]
            kb0 = window_start(blk) * GRID_W // LANES
            kwin = jnp.concatenate([kt_ref[kb0 + c] for c in range(kblocks)], axis=1)
            for h in range(2):
                qh = jnp.where(first_head if h == 0 else jnp.logical_not(first_head), q2, jnp.zeros_like(q2))
                scores.append(_dot(qh, kwin) + bias(h, blk))
        return scores

    def output_stage(group, scores):
        probs = []
        for s in scores:
            p = jnp.exp(s - jnp.max(s, axis=-1, keepdims=True))
            probs.append((p.astype(BF16), jnp.sum(p, axis=-1, keepdims=True)))
        for u, blk in enumerate(group):
            k0 = window_start(blk) * GRID_W
            vwin = v_ref[k0:k0 + NA_K, :]
            outs = [_dot(p, vwin) / l for p, l in probs[2 * u:2 * u + 2]]
            o_ref[blk * NA_Q:(blk + 1) * NA_Q, :] = jnp.where(first_head, outs[0], outs[1]).astype(BF16)

    nblk = rows // NA_ROWS_PER_BLOCK
    groups = [range(g, g + NA_UNROLL) for g in range(0, nblk, NA_UNROLL)]
    scores = score_stage(groups[0])
    for g, group in enumerate(groups):
        ahead = score_stage(groups[g + 1]) if g + 1 < len(groups) else None
        output_stage(group, scores)
        scores = ahead


def _na_attention(qa, kat, va, bias, e, batch, seq):
    n = qa.shape[0]
    rows = seq // GRID_W
    ntiles = bias.shape[2]
    pairs = NA_HEADS // 2
    return pl.pallas_call(
        functools.partial(_na_kernel, rows=rows),
        grid=(pairs, batch),
        in_specs=[
            pl.BlockSpec((seq, LANES), lambda p, b: (b, p)),
            pl.BlockSpec((None, seq // LANES, LANES, LANES), lambda p, b: (b, 0, p, 0)),
            pl.BlockSpec((seq, LANES), lambda p, b: (b, p)),
            pl.BlockSpec((None, 2, ntiles, GRID_W, LANES), lambda p, b: (e * pairs + p, 0, 0, 0, 0)),
        ],
        out_specs=pl.BlockSpec((seq, LANES), lambda p, b: (b, p)),
        out_shape=jax.ShapeDtypeStruct((n, D_A), BF16),
        compiler_params=_params("parallel", "parallel"),
        name="na_attention",
    )(qa, kat, va, bias)


def _log_sigmoid(x):
    return jnp.minimum(x, 0.0) - jnp.log1p(jnp.exp(-jnp.abs(x)))


def _mlstm_kernel(q_ref, kt_ref, v_ref, ob_ref, gt_ref, gn_ref, o_ref,
                  h_ref, src_ref, srcmax_ref, gtot_ref, bcols_ref, *, seq):
    L = ML_CHUNK
    nc = seq // L
    assert nc % 2 == 0
    cb = L // LANES
    log_scale = -0.5 * float(np.log(ML_HEAD_DIM))
    row_i = lax.broadcasted_iota(jnp.int32, (L, L), 0)
    col_i = lax.broadcasted_iota(jnp.int32, (L, L), 1)
    ones_col = (lax.broadcasted_iota(jnp.int32, (L, LANES), 1) == 0).astype(BF16)

    def tri_mask(reverse):
        return (col_i >= row_i) if reverse else (col_i <= row_i)

    def stat_row(c, head, reverse):
        return (head * nc + c) * SUBLANES + (2 if reverse else 0)

    def gate_prologue(head):
        hrows = slice(head * SUBLANES, (head + 1) * SUBLANES)
        x = jnp.concatenate(
            [jnp.concatenate([gt_ref[c * cb + j, hrows, :] for j in range(cb)], axis=1) for c in range(nc)],
            axis=0)
        n = nc * SUBLANES
        kind = lax.broadcasted_iota(jnp.int32, (n, 1), 0) % SUBLANES
        logf = _log_sigmoid(x)

        def split3(a):
            hi = a.astype(BF16)
            rest = a - hi.astype(F32)
            mid = rest.astype(BF16)
            return hi, mid, (rest - mid.astype(F32)).astype(BF16)

        pieces = split3(logf)
        pad = jnp.zeros((LANES - n, L), F32)
        pieces_t = split3(jnp.concatenate([logf, pad], axis=0).T)
        lower = tri_mask(False).astype(BF16)
        upper = tri_mask(True).astype(BF16)
        b_rows_f = sum(_dot(p, upper) for p in pieces)
        b_rows_r = sum(_dot(p, lower) for p in pieces)
        b_cols_f = sum(_dot(lower, p) for p in pieces_t)
        b_cols_r = sum(_dot(upper, p) for p in pieces_t)
        up1 = lambda a: pltpu.roll(a, n - 1, axis=0)
        src = x - jnp.where(kind == 0, up1(b_rows_f), up1(b_rows_r))
        rows = pl.ds(head * n, n)
        src_ref[rows, :] = src
        srcmax_ref[rows, :] = jnp.broadcast_to(jnp.max(src, axis=1, keepdims=True), (n, L))
        gtot_ref[rows, :] = up1(jnp.broadcast_to(jnp.sum(logf, axis=1, keepdims=True), (n, L)))
        bcols_ref[2 * head] = b_cols_f
        bcols_ref[2 * head + 1] = b_cols_r

    def stage_load(c, carry, head, reverse):
        rows = slice(c * L, (c + 1) * L)
        lanes = slice(head * ML_HEAD_DIM, (head + 1) * ML_HEAD_DIM)
        b_lane = c * SUBLANES + (3 if reverse else 1)
        st = dict(
            tri=tri_mask(reverse), row=stat_row(c, head, reverse),
            b_col=bcols_ref[2 * head + int(reverse), :, b_lane:b_lane + 1],
            q=q_ref[rows, lanes],
            kt=jnp.concatenate([kt_ref[c * cb + j, lanes, :] for j in range(cb)], axis=1),
            v_ext=jnp.concatenate([v_ref[rows, lanes], ones_col], axis=1),
            c_ext=carry[0], m=carry[1])
        st['qk'] = _dot(st['q'], st['kt'])
        st['qc'] = _dot(st['q'], st['c_ext'].astype(BF16))
        return st

    def stage_gates(st):
        m, slot = st['m'], pl.ds(st['row'], 1)
        src = src_ref[slot, :]
        mm = jnp.maximum(m, srcmax_ref[slot, :])
        st['w_src'] = jnp.exp(src + (log_scale - mm))
        st['w_carry'] = jnp.exp(m - mm)[:, :1]
        st['floor'] = jnp.exp(-(st['b_col'] + mm[:, :1]))
        st['m_new'] = gtot_ref[slot, :] + mm

    def stage_state(st):
        kv = _dot((st['kt'].astype(F32) * st['w_src']).astype(BF16), st['v_ext'])
        return st['w_carry'] * st['c_ext'] + kv, st['m_new']

    def stage_out(st):
        qkw = jnp.where(st['tri'], st['qk'] * st['w_src'], 0.0)
        tot = _dot(qkw.astype(BF16), st['v_ext']) + st['w_carry'] * st['qc']
        num = tot[:, :ML_HEAD_DIM]
        den = tot[:, ML_HEAD_DIM:ML_HEAD_DIM + 1]
        return num / jnp.maximum(jnp.abs(den), st['floor'])

    def finish(c, head, h):
        rows = slice(c * L, (c + 1) * L)
        lanes = slice(head * ML_HEAD_DIM, (head + 1) * ML_HEAD_DIM)
        mu = jnp.mean(h, axis=-1, keepdims=True)
        hc = h - mu
        var = jnp.mean(hc * hc, axis=-1, keepdims=True)
        hn = hc * lax.rsqrt(var + LN_EPS)
        gate = jax.nn.sigmoid(ob_ref[rows, lanes])
        o_ref[rows, lanes] = (gate * hn * gn_ref[head]).astype(BF16)

    def body(ci, carry, second_half):
        streams = []
        for head in range(ML_HEADS_PER_STEP):
            streams.append((ci, head, stage_load(ci, carry[2 * head], head, False)))
            streams.append((nc - 1 - ci, head, stage_load(nc - 1 - ci, carry[2 * head + 1], head, True)))
        for _, _, st in streams:
            stage_gates(st)
        new_carry = tuple(stage_state(st) for _, _, st in streams)
        for c, head, st in streams:
            h = stage_out(st)
            rows = slice(c * L, (c + 1) * L)
            lanes = slice(head * ML_HEAD_DIM, (head + 1) * ML_HEAD_DIM)
            if second_half:
                finish(c, head, h + h_ref[rows, lanes])
            else:
                h_ref[rows, lanes] = h
        return new_carry

    for head in range(ML_HEADS_PER_STEP):
        gate_prologue(head)

    zero = (jnp.zeros((ML_HEAD_DIM, 2 * LANES), F32), jnp.zeros((1, L), F32))
    carry = (zero,) * (2 * ML_HEADS_PER_STEP)
    for ci in range(nc):
        carry = body(ci, carry, second_half=ci >= nc // 2)


def _mlstm(qb, kbt, vb, ob, gt, gn_g, e, batch, seq):
    n = qb.shape[0]
    hps = ML_HEADS_PER_STEP
    width = hps * ML_HEAD_DIM
    tok = lambda b, g: (b, g)
    return pl.pallas_call(
        functools.partial(_mlstm_kernel, seq=seq),
        grid=(batch, ML_HEADS // hps),
        in_specs=[
            pl.BlockSpec((seq, width), tok),
            pl.BlockSpec((None, seq // LANES, width, LANES), lambda b, g: (b, 0, g, 0)),
            pl.BlockSpec((seq, width), tok),
            pl.BlockSpec((seq, width), tok),
            pl.BlockSpec((None, seq // LANES, hps * SUBLANES, LANES), lambda b, g: (b, 0, g, 0)),
            pl.BlockSpec((hps, 1, ML_HEAD_DIM), lambda b, g: (e * (ML_HEADS // hps) + g, 0, 0)),
        ],
        out_specs=pl.BlockSpec((seq, width), tok),
        out_shape=jax.ShapeDtypeStruct((n, D_B), BF16),
        scratch_shapes=[pltpu.VMEM((seq, width), F32)]
        + [pltpu.VMEM((hps * (seq // ML_CHUNK) * SUBLANES, ML_CHUNK), F32)] * 3
        + [pltpu.VMEM((2 * hps, ML_CHUNK, LANES), F32)],
        compiler_params=_params("parallel", "parallel"),
        name="mlstm",
    )(qb, kbt, vb, ob, gt, gn_g)


def _conv_kernel(x_ref, xp_ref, xn_ref, win_ref, cw_ref, cb_ref, wout_ref, g_ref, b_ref, o_ref,
                 *, tiles_per_seq):
    i = pl.program_id(0)
    D = D_MODEL
    x = x_ref[...]
    xb = x.astype(BF16)
    xe = jnp.concatenate([xp_ref[...], x, xn_ref[...]], axis=0).astype(BF16)
    ue = _dot(xe, win_ref[:, D:2 * D]) * _dot(xe, win_ref[:, 2 * D:])
    ext = PROJ_TM + 2 * SUBLANES
    tile = slice(SUBLANES, SUBLANES + PROJ_TM)
    u = ue[tile]
    u_m1 = pltpu.roll(ue, 1, axis=0)[tile]
    u_p1 = pltpu.roll(ue, ext - 1, axis=0)[tile]
    pos = i % tiles_per_seq
    row = lax.broadcasted_iota(jnp.int32, (PROJ_TM, 1), 0)
    u_m1 = jnp.where(jnp.logical_and(row == 0, pos == 0), 0.0, u_m1)
    u_p1 = jnp.where(jnp.logical_and(row == PROJ_TM - 1, pos == tiles_per_seq - 1), 0.0, u_p1)
    y = cw_ref[0:1, :] * u_m1 + cw_ref[1:2, :] * u + cw_ref[2:3, :] * u_p1 + cb_ref[...]
    z = (_dot(xb, win_ref[:, :D]) * y).astype(BF16)
    mix = _dot(z, wout_ref[...])
    o_ref[...] = _layer_norm(DEEPNORM_ALPHA * x + mix, g_ref[...], b_ref[...])


def _conv_mixer_ln(x, w_in, conv_w, conv_b, w_out, ln_g, ln_b, o, ln_idx, seq):
    n = x.shape[0]
    tiles_per_seq = seq // PROJ_TM
    rb = PROJ_TM // SUBLANES
    last_rb = n // SUBLANES - 1
    tok = lambda i: (i, 0)
    return pl.pallas_call(
        functools.partial(_conv_kernel, tiles_per_seq=tiles_per_seq),
        grid=(n // PROJ_TM,),
        in_specs=[
            pl.BlockSpec((PROJ_TM, D_MODEL), tok),
            pl.BlockSpec((SUBLANES, D_MODEL), lambda i: (jnp.maximum(i * rb - 1, 0), 0)),
            pl.BlockSpec((SUBLANES, D_MODEL), lambda i: (jnp.minimum((i + 1) * rb, last_rb), 0)),
            _resident((None, D_MODEL, 3 * D_MODEL), lambda i: (o, 0, 0)),
            pl.BlockSpec((None, 3, D_MODEL), lambda i: (o, 0, 0)),
            pl.BlockSpec((None, 1, D_MODEL), lambda i: (o, 0, 0)),
            _resident((None, D_MODEL, D_MODEL), lambda i: (o, 0, 0)),
            pl.BlockSpec((None, 1, D_MODEL), lambda i: (ln_idx, 0, 0)),
            pl.BlockSpec((None, 1, D_MODEL), lambda i: (ln_idx, 0, 0)),
        ],
        out_specs=pl.BlockSpec((PROJ_TM, D_MODEL), tok),
        out_shape=jax.ShapeDtypeStruct((n, D_MODEL), F32),
        compiler_params=_params("parallel"),
        name="conv_mixer_ln",
    )(x, x, x, w_in, conv_w, conv_b, w_out, ln_g, ln_b)


def _prep_ab_weights(ab_w_in, ab_gate_b):
    n_even = ab_w_in.shape[0]
    w = ab_w_in
    cut = lambda lo, hi: w[:, :, lo:hi]
    qa, ka, va = cut(0, D_A), cut(D_A, 2 * D_A), cut(2 * D_A, 3 * D_A)
    o0 = 3 * D_A
    qb, kb, vb, ob = (cut(o0, o0 + D_B), cut(o0 + D_B, o0 + 2 * D_B),
                      cut(o0 + 2 * D_B, o0 + 3 * D_B), cut(o0 + 3 * D_B, o0 + 4 * D_B))
    gates = cut(o0 + 4 * D_B, o0 + 4 * D_B + 4 * ML_HEADS)
    gates = gates.reshape(n_even, D_MODEL, 4, ML_HEADS).transpose(0, 3, 2, 1)
    gates = jnp.pad(gates, ((0, 0), (0, 0), (0, SUBLANES - 4), (0, 0))).reshape(n_even, _GATE_ROWS, D_MODEL)
    wnn = jnp.concatenate([qa, va, qb, vb, ob], axis=2).astype(BF16)
    wnt = jnp.concatenate([ka.transpose(0, 2, 1), kb.transpose(0, 2, 1), gates], axis=1).astype(BF16)
    gb = jnp.pad(ab_gate_b.astype(F32).transpose(0, 2, 1), ((0, 0), (0, 0), (0, SUBLANES - 4)))
    return wnn, wnt, gb.reshape(n_even, _GATE_ROWS, 1)


def kernel(x, ln_g, ln_b, ffn_w_in, ffn_w_out, ab_w_in, ab_gate_b, na_rpb, ml_gn_g, ab_w_out,
           sc_w_in, sc_conv_w, sc_conv_b, sc_w_out):
    batch, seq, d = x.shape
    rows = seq // GRID_W
    h = x.reshape(batch * seq, d)
    lng =ln_g.astype(F32).reshape(DEPTH * 3, 1, d)
    lnb = ln_b.astype(F32).reshape(DEPTH * 3, 1, d)
    wnn, wnt, gate_b = _prep_ab_weights(ab_w_in, ab_gate_b)
    ab_out = ab_w_out.astype(BF16)
    gn = ml_gn_g.astype(F32).reshape(-1, 1, ML_HEAD_DIM)
    sc_in = sc_w_in.astype(BF16)
    sc_out = sc_w_out.astype(BF16)
    conv_b = sc_conv_b.astype(F32).reshape(-1, 1, d)
    conv_w = sc_conv_w.astype(F32)
    na_bias = _na_bias_tables(na_rpb.astype(F32), rows)
    na_bias = na_bias.reshape(-1, 2, *na_bias.shape[2:])

    w_in, w_out = ffn_w_in[0, 0].astype(BF16), ffn_w_out[0, 0].astype(BF16)

    def ffn(h, w_in, w_out, layer, k, mix=None):
        nxt = 2 * layer + k + 1
        if nxt == 2 * DEPTH:
            (h,) = _ffn_ln(h, w_in, w_out, lng, lnb, layer * 3 + 2 * k, mix=mix)
            return h, None, None
        return _ffn_ln(h, w_in, w_out, lng, lnb, layer * 3 + 2 * k,
                       (ffn_w_in, ffn_w_out, nxt // 2, nxt % 2), mix)

    for layer in range(DEPTH):
        h, w_in, w_out = ffn(h, w_in, w_out, layer, 0)
        if layer % 2 == 0:
            e = layer // 2
            qa, va, qb, vb, ob, kat, kbt, gt = _ab_proj(h, wnn, wnt, gate_b, e, batch, seq)
            ya = _na_attention(qa, kat, va, na_bias, e, batch, seq)
            yb = _mlstm(qb, kbt, vb, ob, gt, gn, e, batch, seq)
            mix = (ya, yb, ab_out, e, layer * 3 + 1)
        else:
            o = layer // 2
            h = _conv_mixer_ln(h, sc_in, conv_w, conv_b, sc_out, lng, lnb, o, layer * 3 + 1, seq)
            mix = None
        h, w_in, w_out = ffn(h, w_in, w_out, layer, 1, mix)
    return h.reshape(batch, seq, d)
```
